```python
import math, functools
import jax, jax.numpy as jnp
from jax import lax
import numpy as np

D_MODEL = 1024
BATCH = 16
SEQ = 2048
DEPTH = 2
DEC_BATCH = 32
DEC_SEQ = 4
PAST_LEN = 16384
PAGE_SIZE = 128

HD = 64
H_A = 6
G_A = 2
HG_A = H_A // G_A
H_B = 6
H_C = 4
DH_C = 32
DV_C = 2 * DH_C
MIX_WIDTH = (H_A + H_B) * HD + H_C * DV_C
ROPE_THETA = 500000.0
L_CMP = 32
STRIDE = 16
L_SEL = 64
SEL_RATIO = L_SEL // STRIDE
N_SEL_TOP = 16
N_LOCAL = 2
WINDOW = 512
CMP_HID = 2 * HD
MOBA_BLOCK = 256
MOBA_TOPK = 3
D_FF = 256 * ((8 * D_MODEL // 3 + 255) // 256)
N_EXPERTS = 8
TOP_K = 2
D_PLE = 256
N_DENSE = (DEPTH + 1) // 2
N_MOE = DEPTH // 2
Q_BLK = 128
G_QBLK = 32
EPS = 1e-6
BIG = 1e9
SPLIT_SIZES = (H_A * HD, 2 * G_A * HD, 2 * G_A * HD, 2 * G_A * HD, 3 * H_A,
               H_B * HD, H_B * HD, H_B * HD, H_C * DV_C, H_C * DV_C, H_C * DV_C)
N_IN = sum(SPLIT_SIZES)

kernel_name = 'hymba_nsa_moba_diff_decoder_step'


def rms_norm(x, g):
    xf = x.astype(jnp.float32)
    y = xf * lax.rsqrt(jnp.mean(xf * xf, axis=-1, keepdims=True) + EPS)
    return y.astype(x.dtype) * g


def rope(x, pos):
    rot = x.shape[-1] // 4
    half = rot // 2
    inv = ROPE_THETA ** (-2.0 * jnp.arange(half, dtype=jnp.float32) / rot)
    ang = pos.astype(jnp.float32)[:, None] * inv
    ang = ang.reshape((pos.shape[0],) + (1,) * (x.ndim - 3) + (half,))
    cos, sin = jnp.cos(ang), jnp.sin(ang)
    x1 = x[..., :half].astype(jnp.float32)
    x2 = x[..., half:rot].astype(jnp.float32)
    rotated = jnp.concatenate([x1 * cos - x2 * sin, x2 * cos + x1 * sin], axis=-1).astype(x.dtype)
    return jnp.concatenate([rotated, x[..., rot:]], axis=-1)


def masked_softmax(s, valid):
    s = jnp.where(valid, s.astype(jnp.float32), -1e30)
    m = jnp.max(s, axis=-1, keepdims=True)
    e = jnp.where(valid, jnp.exp(s - m), 0.0)
    return e / jnp.maximum(jnp.sum(e, axis=-1, keepdims=True), 1e-30)


def swiglu(f, wg, wu, wd):
    return (jax.nn.silu(f @ wg) * (f @ wu)) @ wd


def moe_swiglu(f, router, wg, wu, wd):
    logits = jnp.einsum('btd,de->bte', f.astype(jnp.float32), router.astype(jnp.float32))
    top_v, top_i = lax.top_k(logits, TOP_K)
    gate = jax.nn.softmax(top_v, axis=-1)
    out = jnp.zeros_like(f)
    for e in range(N_EXPERTS):
        w_e = jnp.sum(jnp.where(top_i == e, gate, 0.0), axis=-1).astype(f.dtype)[..., None]
        out = out + w_e * swiglu(f, wg[e], wu[e], wd[e])
    return out


def split_cols(z):
    out, start = [], 0
    for n in SPLIT_SIZES:
        out.append(z[..., start:start + n])
        start += n
    return out


def gather_pages(pool, page_table):
    g = pool[page_table]
    return g.reshape((g.shape[0], g.shape[1] * g.shape[2]) + g.shape[3:])


def join_rows(past_rows, new, block):
    parts = [new] if past_rows is None else [past_rows, new]
    length = sum(p.shape[1] for p in parts)
    pad = (-length) % block
    if pad:
        parts.append(jnp.zeros((new.shape[0], pad) + new.shape[2:], new.dtype))
    return parts[0] if len(parts) == 1 else jnp.concatenate(parts, axis=1)


def map_query_chunks(fn, q_args, s_args, qpos, chunk):
    B, Tq = q_args[0].shape[:2]
    c = chunk if Tq % chunk == 0 else Tq
    n = Tq // c
    qc = [a.reshape((B * n, c) + a.shape[2:]) for a in q_args]
    pc = qpos.reshape(n, c)

    def step(i):
        b = i // n
        return fn([a[i] for a in qc], pc[i % n], [a[b] for a in s_args])

    out = lax.map(step, jnp.arange(B * n, dtype=jnp.int32))
    return out.reshape((B, Tq) + out.shape[2:])


def nsa_compress(x, w1, w2, pe):
    B, L = x.shape[:2]
    n = (L - L_CMP) // STRIDE + 1
    halves = x[:, :(n + 1) * STRIDE].reshape(B, n + 1, STRIDE, G_A, HD)
    first = jnp.einsum('bnsgd,sdh->bngh', halves, w1[:STRIDE])[:, :-1]
    second = jnp.einsum('bnsgd,sdh->bngh', halves, w1[STRIDE:])[:, 1:]
    pe_term = jnp.einsum('sd,sdh->h', pe, w1)
    return jax.nn.silu(first + second + pe_term) @ w2


def nsa_chunk(q_args, qpos, s_args, win_k0):
    q, gate = q_args
    k_cmp, v_cmp, sel_blocks, win = s_args
    c = q.shape[0]
    scale = HD ** -0.5
    n_cmp = k_cmp.shape[0]
    s = jnp.einsum('cgzd,ngd->cgzn', q, k_cmp) * scale
    end = jnp.arange(n_cmp) * STRIDE + (L_CMP - 1)
    p_cmp = masked_softmax(s, (end[None, :] <= qpos[:, None])[:, None, None, :])
    o_cmp = jnp.einsum('cgzn,ngd->cgzd', p_cmp.astype(v_cmp.dtype), v_cmp)
    n_sel = sel_blocks.shape[0]
    imp = jnp.pad(jnp.sum(p_cmp, axis=2), ((0, 0), (0, 0), (0, SEL_RATIO * n_sel - n_cmp)))
    imp = imp.reshape(c, G_A, n_sel, SEL_RATIO)
    imp = jnp.sum(imp, axis=-1) + jnp.pad(imp[:, :, :-1, SEL_RATIO - 1], ((0, 0), (0, 0), (1, 0)))
    qblk = qpos // L_SEL
    blk = jnp.arange(n_sel)
    causal = (blk[None, :] <= qblk[:, None])[:, None, :]
    forced = ((blk[None, :] == 0) | (blk[None, :] > qblk[:, None] - N_LOCAL))[:, None, :]
    score = jnp.where(causal, jnp.where(forced, BIG, imp), -BIG)
    k_s = min(N_SEL_TOP, n_sel)
    _, idx = lax.top_k(score, k_s)
    ok = idx <= qblk[:, None, None]
    g_ix = jnp.arange(G_A)[None, :, None]
    kg = sel_blocks[:, :, 0].transpose(2, 0, 1, 3)[g_ix, idx]
    vg = sel_blocks[:, :, 1].transpose(2, 0, 1, 3)[g_ix, idx]
    s2 = jnp.einsum('cgzd,cgkld->cgzkl', q, kg).reshape(c, G_A, HG_A, k_s * L_SEL) * scale
    kpos = idx[..., None] * L_SEL + jnp.arange(L_SEL)
    valid2 = (ok[..., None] & (kpos <= qpos[:, None, None, None])).reshape(c, G_A, 1, k_s * L_SEL)
    p2 = masked_softmax(s2, valid2)
    o_sel = jnp.einsum('cgzn,cgnd->cgzd', p2.astype(vg.dtype), vg.reshape(c, G_A, k_s * L_SEL, HD))
    lw = win.shape[0]
    band = min(lw, c + WINDOW)
    start = jnp.clip(qpos[0] - WINDOW - win_k0, 0, lw - band)
    wb = lax.dynamic_slice_in_dim(win, start, band, axis=0)
    kpos_w = win_k0 + start + jnp.arange(band)
    valid3 = ((kpos_w[None, :] <= qpos[:, None]) & (kpos_w[None, :] >= qpos[:, None] - WINDOW)
              & (kpos_w[None, :] >= 0))
    s3 = jnp.einsum('cgzd,lgd->cgzl', q, wb[:, 0]) * scale
    p3 = masked_softmax(s3, valid3[:, None, None, :])
    o_win = jnp.einsum('cgzl,lgd->cgzd', p3.astype(wb.dtype), wb[:, 1])
    return (gate[:, 0, ..., None] * o_cmp + gate[:, 1, ..., None] * o_sel
            + gate[:, 2, ..., None] * o_win)


def moba_chunk(q_args, qpos, s_args):
    (q,) = q_args
    blocks, means = s_args
    c = q.shape[0]
    n_blk = blocks.shape[0]
    k_m = min(MOBA_TOPK, n_blk)
    own = qpos // MOBA_BLOCK
    gate = jnp.einsum('chd,nhd->chn', q.astype(jnp.float32), means)
    past = (jnp.arange(n_blk)[None, :] < own[:, None])[:, None, :]
    _, idx = lax.top_k(jnp.where(past, gate, -BIG), k_m)
    ok = idx < own[:, None, None]
    kb, vb = blocks[:, :, 0], blocks[:, :, 1]
    h_ix = jnp.arange(H_B)[None, :, None]
    kg = kb.transpose(2, 0, 1, 3)[h_ix, idx]
    vg = vb.transpose(2, 0, 1, 3)[h_ix, idx]
    ko, vo = kb[own], vb[own]
    n_g = k_m * MOBA_BLOCK
    s = jnp.concatenate([jnp.einsum('chd,chkmd->chkm', q, kg).reshape(c, H_B, n_g),
                         jnp.einsum('chd,cmhd->chm', q, ko)], axis=-1) * HD ** -0.5
    own_pos = own[:, None] * MOBA_BLOCK + jnp.arange(MOBA_BLOCK)
    valid = jnp.concatenate([
        jnp.broadcast_to(ok[..., None], (c, H_B, k_m, MOBA_BLOCK)).reshape(c, H_B, n_g),
        jnp.broadcast_to((own_pos <= qpos[:, None])[:, None, :], (c, H_B, MOBA_BLOCK))], axis=-1)
    p = masked_softmax(s, valid).astype(vo.dtype)
    return (jnp.einsum('chn,chnd->chd', p[..., :n_g], vg.reshape(c, H_B, n_g, HD))
            + jnp.einsum('chm,cmhd->chd', p[..., n_g:], vo))


def diff_chunk(q_args, qpos, s_args, lam, lam_init, norm_g):
    (q,) = q_args
    (kv,) = s_args
    L = kv.shape[0]
    k = kv[:, 0].reshape(L, H_C, 2, DH_C)
    v = kv[:, 1]
    s = jnp.einsum('chjd,lhjd->chjl', q, k) * DH_C ** -0.5
    valid = (jnp.arange(L)[None, :] <= qpos[:, None])[:, None, None, :]
    a = masked_softmax(s, valid)
    w = (a[:, :, 0] - lam * a[:, :, 1]).astype(v.dtype)
    o = jnp.einsum('chl,lhd->chd', w, v)
    return rms_norm(o, norm_g) * (1.0 - lam_init)


def token_mix(z, t0, past, prm, i):
    B, Tq = z.shape[:2]
    pos = t0 + jnp.arange(Tq, dtype=jnp.int32)
    qa, kv_c, kv_s, kv_w, ga, qb, kb, vb, qc, kc, vc = split_cols(z)
    qa = rope(qa.reshape(B, Tq, G_A, HG_A, HD), pos)

    def nsa_kv(t):
        t = t.reshape(B, Tq, 2, G_A, HD)
        return jnp.stack([rope(t[:, :, 0], pos), t[:, :, 1]], axis=2)

    new_cmp, new_sel, new_win = nsa_kv(kv_c), nsa_kv(kv_s), nsa_kv(kv_w)
    gates = jax.nn.sigmoid(ga.reshape(B, Tq, 3, G_A, HG_A))
    qb = rope(qb.reshape(B, Tq, H_B, HD), pos)
    new_moba = jnp.stack([rope(kb.reshape(B, Tq, H_B, HD), pos), vb.reshape(B, Tq, H_B, HD)], axis=2)
    qc = rope(qc.reshape(B, Tq, H_C, 2, DH_C), pos)
    kc = rope(kc.reshape(B, Tq, H_C, 2, DH_C), pos).reshape(B, Tq, H_C, DV_C)
    new_diff = jnp.stack([kc, vc.reshape(B, Tq, H_C, DV_C)], axis=2)

    if past is None:
        full_cmp = new_cmp
        full_sel = join_rows(None, new_sel, L_SEL)
        full_moba = join_rows(None, new_moba, MOBA_BLOCK)
        full_diff = new_diff
        win_all = jnp.pad(new_win, ((0, 0), (WINDOW, 0), (0, 0), (0, 0), (0, 0)))
        n_real = Tq
    else:
        pt = past['page_table']
        full_cmp = join_rows(gather_pages(past['nsa_cmp'][i], pt), new_cmp, 1)
        full_sel = join_rows(gather_pages(past['nsa_sel'][i], pt), new_sel, L_SEL)
        full_moba = join_rows(gather_pages(past['moba'][i], pt), new_moba, MOBA_BLOCK)
        full_diff = join_rows(gather_pages(past['diff'][i], pt), new_diff, 1)
        win_all = jnp.concatenate([past['nsa_win'][i], new_win], axis=1)
        n_real = win_all.shape[1]
    new_win_state = win_all[:, win_all.shape[1] - min(WINDOW, n_real):]
    win_k0 = t0 - (win_all.shape[1] - Tq)

    k_cmp = nsa_compress(full_cmp[:, :, 0], prm['nsa_cmp_k_w1'][i], prm['nsa_cmp_k_w2'][i], prm['nsa_cmp_k_pe'][i])
    v_cmp = nsa_compress(full_cmp[:, :, 1], prm['nsa_cmp_v_w1'][i], prm['nsa_cmp_v_w2'][i], prm['nsa_cmp_v_pe'][i])
    sel_blocks = full_sel.reshape((B, full_sel.shape[1] // L_SEL, L_SEL) + full_sel.shape[2:])
    o_a = map_query_chunks(functools.partial(nsa_chunk, win_k0=win_k0), (qa, gates),
                           (k_cmp, v_cmp, sel_blocks, win_all), pos, G_QBLK)
    moba_blocks = full_moba.reshape((B, full_moba.shape[1] // MOBA_BLOCK, MOBA_BLOCK) + full_moba.shape[2:])
    means = jnp.mean(moba_blocks[:, :, :, 0].astype(jnp.float32), axis=2)
    o_b = map_query_chunks(moba_chunk, (qb,), (moba_blocks, means), pos, G_QBLK)
    lam_init = 0.8 - 0.6 * math.exp(-0.3 * i)
    f32 = jnp.float32
    lam = (jnp.exp(jnp.sum(prm['diff_lq1'][i].astype(f32) * prm['diff_lk1'][i].astype(f32)))
           - jnp.exp(jnp.sum(prm['diff_lq2'][i].astype(f32) * prm['diff_lk2'][i].astype(f32))) + lam_init)
    o_c = map_query_chunks(functools.partial(diff_chunk, lam=lam, lam_init=lam_init, norm_g=prm['diff_norm_g'][i]),
                           (qc,), (full_diff,), pos, Q_BLK)
    mix = jnp.concatenate([o_a.reshape(B, Tq, H_A * HD), o_b.reshape(B, Tq, H_B * HD),
                           o_c.reshape(B, Tq, H_C * DV_C)], axis=-1)
    return mix, (new_cmp, new_sel, new_win_state, new_moba, new_diff)


def run_trunk(x, p_emb, t0, past, prm):
    h = x
    layer_states = []
    for i in range(DEPTH):
        a = rms_norm(h, prm['g_attn'][i])
        mix, st = token_mix(a @ prm['w_in'][i], t0, past, prm, i)
        h = h + mix @ prm['w_out'][i]
        f = rms_norm(h, prm['g_ffn'][i])
        j = i // 2
        if i % 2 == 0:
            h = h + swiglu(f, prm['ffn_w_gate'][j], prm['ffn_w_up'][j], prm['ffn_w_down'][j])
        else:
            h = h + moe_swiglu(f, prm['moe_router'][j], prm['moe_w_gate'][j], prm['moe_w_up'][j], prm['moe_w_down'][j])
        gate = jax.nn.sigmoid(rms_norm(h, prm['g_ple'][i]) @ prm['ple_gate'][i])
        h = h + gate * (p_emb[i] @ prm['ple_proj'][i])
        layer_states.append(st)
    states = [jnp.stack(s, axis=0) for s in zip(*layer_states)]
    return rms_norm(h, prm['g_final']), states


def setup_inputs(seed: int = 0) -> dict:
    key = jax.random.key(seed)
    keys = iter(jax.random.split(key, 48))

    def nrm(shape, scale=1.0):
        return jax.random.normal(next(keys), shape, jnp.float32) * scale

    def gain(shape):
        return 1.0 + nrm(shape, 0.02)

    n_pages = PAST_LEN // PAGE_SIZE
    n_used = DEC_BATCH * n_pages
    n_pool = n_used + max(1, n_used // 4)
    w_buf = min(WINDOW, PAST_LEN)
    page_table = jax.random.permutation(next(keys), n_pool)[:n_used].reshape(DEC_BATCH, n_pages).astype(jnp.int32)
    return {
        'x_prompt': nrm((BATCH, SEQ, D_MODEL)),
        'x_sample': nrm((DEC_BATCH, DEC_SEQ, D_MODEL)),
        'cache_nsa_cmp': nrm((DEPTH, n_pool, PAGE_SIZE, 2, G_A, HD)),
        'cache_nsa_sel': nrm((DEPTH, n_pool, PAGE_SIZE, 2, G_A, HD)),
        'cache_nsa_win': nrm((DEPTH, DEC_BATCH, w_buf, 2, G_A, HD)),
        'cache_moba': nrm((DEPTH, n_pool, PAGE_SIZE, 2, H_B, HD)),
        'cache_diff': nrm((DEPTH, n_pool, PAGE_SIZE, 2, H_C, DV_C)),
        'page_table': page_table,
        'p_prompt': nrm((DEPTH, BATCH, SEQ, D_PLE)),
        'p_sample': nrm((DEPTH, DEC_BATCH, DEC_SEQ, D_PLE)),
        'g_attn': gain((DEPTH, D_MODEL)),
        'w_in': nrm((DEPTH, D_MODEL, N_IN), D_MODEL ** -0.5),
        'w_out': nrm((DEPTH, MIX_WIDTH, D_MODEL), MIX_WIDTH ** -0.5),
        'nsa_cmp_k_w1': nrm((DEPTH, L_CMP, HD, CMP_HID), (L_CMP * HD) ** -0.5),
        'nsa_cmp_k_w2': nrm((DEPTH, CMP_HID, HD), CMP_HID ** -0.5),
        'nsa_cmp_k_pe': nrm((DEPTH, L_CMP, HD), 0.1),
        'nsa_cmp_v_w1': nrm((DEPTH, L_CMP, HD, CMP_HID), (L_CMP * HD) ** -0.5),
        'nsa_cmp_v_w2': nrm((DEPTH, CMP_HID, HD), CMP_HID ** -0.5),
        'nsa_cmp_v_pe': nrm((DEPTH, L_CMP, HD), 0.1),
        'diff_lq1': nrm((DEPTH, DH_C), 0.1),
        'diff_lk1': nrm((DEPTH, DH_C), 0.1),
        'diff_lq2': nrm((DEPTH, DH_C), 0.1),
        'diff_lk2': nrm((DEPTH, DH_C), 0.1),
        'diff_norm_g': gain((DEPTH, DV_C)),
        'g_ffn': gain((DEPTH, D_MODEL)),
        'ffn_w_gate': nrm((N_DENSE, D_MODEL, D_FF), D_MODEL ** -0.5),
        'ffn_w_up': nrm((N_DENSE, D_MODEL, D_FF), D_MODEL ** -0.5),
        'ffn_w_down': nrm((N_DENSE, D_FF, D_MODEL), D_FF ** -0.5),
        'moe_router': nrm((N_MOE, D_MODEL, N_EXPERTS), D_MODEL ** -0.5),
        'moe_w_gate': nrm((N_MOE, N_EXPERTS, D_MODEL, D_FF), D_MODEL ** -0.5),
        'moe_w_up': nrm((N_MOE, N_EXPERTS, D_MODEL, D_FF), D_MODEL ** -0.5),
        'moe_w_down': nrm((N_MOE, N_EXPERTS, D_FF, D_MODEL), D_FF ** -0.5),
        'ple_gate': nrm((DEPTH, D_MODEL, D_MODEL), D_MODEL ** -0.5),
        'ple_proj': nrm((DEPTH, D_PLE, D_MODEL), D_PLE ** -0.5),
        'g_ple': gain((DEPTH, D_MODEL)),
        'g_final': gain((D_MODEL,)),
    }


def reference(x_prompt, x_sample, cache_nsa_cmp, cache_nsa_sel, cache_nsa_win, cache_moba, cache_diff,
              page_table, p_prompt, p_sample, g_attn, w_in, w_out,
              nsa_cmp_k_w1, nsa_cmp_k_w2, nsa_cmp_k_pe, nsa_cmp_v_w1, nsa_cmp_v_w2, nsa_cmp_v_pe,
              diff_lq1, diff_lk1, diff_lq2, diff_lk2, diff_norm_g, g_ffn,
              ffn_w_gate, ffn_w_up, ffn_w_down, moe_router, moe_w_gate, moe_w_up, moe_w_down,
              ple_gate, ple_proj, g_ple, g_final):
    prm = dict(g_attn=g_attn, w_in=w_in, w_out=w_out,
               nsa_cmp_k_w1=nsa_cmp_k_w1, nsa_cmp_k_w2=nsa_cmp_k_w2, nsa_cmp_k_pe=nsa_cmp_k_pe,
               nsa_cmp_v_w1=nsa_cmp_v_w1, nsa_cmp_v_w2=nsa_cmp_v_w2, nsa_cmp_v_pe=nsa_cmp_v_pe,
               diff_lq1=diff_lq1, diff_lk1=diff_lk1, diff_lq2=diff_lq2, diff_lk2=diff_lk2,
               diff_norm_g=diff_norm_g, g_ffn=g_ffn,
               ffn_w_gate=ffn_w_gate, ffn_w_up=ffn_w_up, ffn_w_down=ffn_w_down,
               moe_router=moe_router, moe_w_gate=moe_w_gate, moe_w_up=moe_w_up, moe_w_down=moe_w_down,
               ple_gate=ple_gate, ple_proj=ple_proj, g_ple=g_ple, g_final=g_final)
    y_prompt, (cmp_p, sel_p, win_p, moba_p, diff_p) = run_trunk(x_prompt, p_prompt, 0, None, prm)
    past_len = page_table.shape[1] * cache_nsa_cmp.shape[2]
    past = dict(nsa_cmp=cache_nsa_cmp, nsa_sel=cache_nsa_sel, nsa_win=cache_nsa_win,
                moba=cache_moba, diff=cache_diff, page_table=page_table)
    y_sample, (cmp_s, sel_s, win_s, moba_s, diff_s) = run_trunk(x_sample, p_sample, past_len, past, prm)
    return (y_prompt, y_sample, cmp_p, sel_p, win_p, moba_p, diff_p, cmp_s, sel_s, win_s, moba_s, diff_s)
```

```python
import functools
import math

import jax
import jax.numpy as jnp
from jax import lax
from jax.experimental import pallas as pl
from jax.experimental.pallas import tpu as pltpu

F32 = jnp.float32
BF16 = jnp.bfloat16
I32 = jnp.int32

HD = 64
H_A = 6
G_A = 2
HG_A = H_A // G_A
H_B = 6
H_C = 4
DH_C = 32
DV_C = 2 * DH_C
ROPE_THETA = 500000.0
L_CMP = 32
STRIDE = 16
L_SEL = 64
SEL_RATIO = L_SEL // STRIDE
N_SEL_TOP = 16
N_LOCAL = 2
WINDOW = 512
CMP_HID = 2 * HD
MOBA_BLOCK = 256
MOBA_TOPK = 3
TOP_K = 2
EPS = 1e-6
BIG = 1e9
NEG = -1e30

LANES = 128
SUBLANES = 8
VMEM_LIMIT_BYTES = 56 * 1024 * 1024

QA0, QB0, QC0, GA0 = 0, H_A * HD, (H_A + H_B) * HD, (H_A + H_B) * HD + H_C * DV_C
NQ = GA0 + LANES
W_NSA = 2 * G_A * HD
W_MOBA = 2 * H_B * HD
W_DIFF = 2 * H_C * DV_C
CMP0, SEL0, WIN0, MOBA0 = 0, W_NSA, 2 * W_NSA, 3 * W_NSA
DIFF0 = MOBA0 + W_MOBA
NKV = DIFF0 + W_DIFF

HIGHEST = lax.Precision.HIGHEST
NT = (((1,), (1,)), ((), ()))


def _cparams(n_grid):
    return pltpu.CompilerParams(dimension_semantics=("arbitrary",) * n_grid,
                                vmem_limit_bytes=VMEM_LIMIT_BYTES)


def _dot(a, b):
    return jnp.dot(a, b, preferred_element_type=F32)


def _dot_nt(a, b):
    return lax.dot_general(a, b, NT, preferred_element_type=F32)


def _sigmoid(x):
    return 1.0 / (1.0 + jnp.exp(-x))


def _rms(x, g):
    return (x * lax.rsqrt(jnp.mean(x * x, axis=-1, keepdims=True) + EPS)) * g


def _topk_mask(score, k, n_valid):
    lane = lax.broadcasted_iota(I32, score.shape, 1)
    rank = jnp.zeros(score.shape, I32)
    for i in range(n_valid):
        col = jnp.broadcast_to(score[:, i:i + 1], score.shape)
        beats = (col > score) | ((col == score) & (lane > i))
        rank = rank + beats.astype(I32)
    return (rank < k) & (lane < n_valid)


def _proj_in_kernel(x_ref, g_ref, wq_ref, wkv_ref, c64_ref, sa64_ref, sb64_ref, c32_ref, sa32_ref, sb32_ref,
                    ck_ref, sk_ref, ck32_ref, sk32_ref, zq_ref, cmp_ref, sel_ref, win_ref, moba_ref, diff_ref):
    ab = _rms(x_ref[...], g_ref[...]).astype(BF16)

    def rope_q(z, c, sa, sb, half):
        return z * c + pltpu.roll(z, half, 1) * sa + pltpu.roll(z, LANES - half, 1) * sb

    for c0, width, kind in ((QA0, H_A * HD, 64), (QB0, H_B * HD, 64), (QC0, H_C * DV_C, 32), (GA0, LANES, 0)):
        z = _dot(ab, wq_ref[:, c0:c0 + width])
        for k in range(width // LANES):
            zc = z[:, k * LANES:(k + 1) * LANES]
            if kind == 64:
                zc = rope_q(zc, c64_ref[...], sa64_ref[...], sb64_ref[...], 8)
            elif kind == 32:
                zc = rope_q(zc, c32_ref[...], sa32_ref[...], sb32_ref[...], 4)
            else:
                zc = _sigmoid(zc)
            zq_ref[:, c0 + k * LANES:c0 + (k + 1) * LANES] = zc

    ck, sk, ck32, sk32 = ck_ref[...], sk_ref[...], ck32_ref[...], sk32_ref[...]

    def rope_k64(zh):
        x1, x2 = zh[0:8], zh[8:16]
        return jnp.concatenate([x1 * ck - x2 * sk, x2 * ck + x1 * sk, zh[16:HD]], axis=0)

    def rope_k32(zh):
        x = zh[0:8]
        return jnp.concatenate([x * ck32 + pltpu.roll(x, 4, 0) * sk32, zh[8:DH_C]], axis=0)

    for r0, width, o_ref, unit, n_rot in ((CMP0, W_NSA, cmp_ref, HD, G_A), (SEL0, W_NSA, sel_ref, HD, G_A),
                                          (WIN0, W_NSA, win_ref, HD, G_A), (MOBA0, W_MOBA, moba_ref, HD, H_B),
                                          (DIFF0, W_DIFF, diff_ref, DH_C, 2 * H_C)):
        z = _dot_nt(wkv_ref[r0:r0 + width, :], ab)
        parts = []
        for u in range(n_rot):
            zh = z[u * unit:(u + 1) * unit]
            parts.append(rope_k64(zh) if unit == HD else rope_k32(zh))
        parts.append(z[n_rot * unit:width])
        o_ref[0] = jnp.concatenate(parts, axis=0)


def _rope_tables(pos):
    posf = pos.astype(F32)[:, None]

    def ang(rot):
        half = rot // 2
        inv = ROPE_THETA ** (-2.0 * jnp.arange(half, dtype=F32) / rot)
        a = posf * inv
        return jnp.cos(a), jnp.sin(a)

    def token_major(period, cos, sin):
        half = cos.shape[1]
        lane = jnp.arange(LANES) % period
        idx = jnp.where(lane < half, lane, lane - half) % half
        lo, hi = lane < half, (lane >= half) & (lane < 2 * half)
        c = jnp.where(lo | hi, cos[:, idx], 1.0)
        sa = jnp.where(hi, sin[:, idx], 0.0)
        sb = jnp.where(lo, -sin[:, idx], 0.0)
        return c, sa, sb

    cos64, sin64 = ang(HD // 4)
    cos32, sin32 = ang(DH_C // 4)
    t64 = token_major(HD, cos64, sin64)
    t32 = token_major(DH_C, cos32, sin32)
    ck, sk = cos64.T, sin64.T
    ck32 = jnp.concatenate([cos32, cos32], axis=1).T
    sk32 = jnp.concatenate([-sin32, sin32], axis=1).T
    return t64 + t32 + (ck, sk, ck32, sk32)


def _proj_in(h, g, wq, wkv, tables, n_seq, t_seq, tm):
    m, d = h.shape
    per = t_seq // tm
    row = lambda i: (i, 0)
    tab = lambda i: (i % per, 0)
    tabk = lambda i: (0, i % per)
    fm = lambda i: (i // per, 0, i % per)
    full = lambda i: (0, 0)
    in_specs = [pl.BlockSpec((tm, d), row), pl.BlockSpec((1, d), full),
                pl.BlockSpec((d, NQ), full), pl.BlockSpec((NKV, d), full)]
    in_specs += [pl.BlockSpec((tm, LANES), tab)] * 6 + [pl.BlockSpec((SUBLANES, tm), tabk)] * 4
    widths = (W_NSA, W_NSA, W_NSA, W_MOBA, W_DIFF)
    out_specs = [pl.BlockSpec((tm, NQ), row)] + [pl.BlockSpec((1, w, tm), fm) for w in widths]
    out_shape = [jax.ShapeDtypeStruct((m, NQ), F32)] + [jax.ShapeDtypeStruct((n_seq, w, t_seq), F32) for w in widths]
    return pl.pallas_call(_proj_in_kernel, grid=(m // tm,), in_specs=in_specs, out_specs=out_specs,
                          out_shape=out_shape, compiler_params=_cparams(1), name="proj_in")(
        h, g.reshape(1, d), wq, wkv, *tables)


def _compress_kernel(x_ref, w1_ref, w1f_ref, pe_ref, w2_ref, o_ref):
    nh = x_ref.shape[3]
    a = _dot(x_ref[0, 0, 0].astype(BF16), w1_ref[0])
    pe_term = _dot(pe_ref[0].astype(BF16), w1f_ref[0])[0:1]
    hid = a[:, :CMP_HID] + pltpu.roll(a[:, CMP_HID:], nh - 1, 0) + pe_term
    act = hid * _sigmoid(hid)
    o_ref[0, 0, 0] = _dot(act.astype(BF16), w2_ref[0])


def _compress(x, w1cat, w1flat, pe8, w2):
    b, _, g, nh, kd = x.shape
    return pl.pallas_call(
        _compress_kernel, grid=(b, 2, g),
        in_specs=[pl.BlockSpec((1, 1, 1, nh, kd), lambda bi, kv, gi: (bi, kv, gi, 0, 0)),
                  pl.BlockSpec((1, kd, 2 * CMP_HID), lambda bi, kv, gi: (kv, 0, 0)),
                  pl.BlockSpec((1, 2 * kd, CMP_HID), lambda bi, kv, gi: (kv, 0, 0)),
                  pl.BlockSpec((1, SUBLANES, 2 * kd), lambda bi, kv, gi: (kv, 0, 0)),
                  pl.BlockSpec((1, CMP_HID, HD), lambda bi, kv, gi: (kv, 0, 0))],
        out_specs=pl.BlockSpec((1, 1, 1, nh, HD), lambda bi, kv, gi: (bi, kv, gi, 0, 0)),
        out_shape=jax.ShapeDtypeStruct((b, 2, g, nh, HD), F32),
        compiler_params=_cparams(3), name="nsa_compress")(x, w1cat, w1flat, pe8, w2)


def _cmp_attn_kernel(cfg, q_ref, ga_ref, kv_ref, o_ref, bm_ref):
    tq, n_cmp, n_sel, t0 = cfg["tq"], cfg["n_cmp"], cfg["n_sel"], cfg["t0"]
    ncp, nsp = kv_ref.shape[3], bm_ref.shape[3]
    i = pl.program_id(1)
    qpos = t0 + i * tq + lax.broadcasted_iota(I32, (tq, 1), 0)
    q = q_ref[0]
    gates = ga_ref[0]
    n_idx = lax.broadcasted_iota(I32, (1, ncp), 1)
    valid = ((n_idx * STRIDE + (L_CMP - 1)) <= qpos) & (n_idx < n_cmp)
    nn = lax.broadcasted_iota(I32, (ncp, nsp), 0)
    jj = lax.broadcasted_iota(I32, (ncp, nsp), 1)
    blk = lax.shift_right_logical(nn, 2)
    overlap = ((blk == jj) | (((nn & 3) == 3) & (blk + 1 == jj))) & (nn < n_cmp)
    mmap = overlap.astype(F32)
    lane = lax.broadcasted_iota(I32, (tq, nsp), 1)
    qblk = lax.shift_right_logical(qpos, 6)
    causal = lane <= qblk
    forced = (lane == 0) | (lane > qblk - N_LOCAL)
    scale = HD ** -0.5
    for g in range(G_A):
        kc = kv_ref[0, 0, g]
        vc = kv_ref[0, 1, g].astype(BF16)
        imp = jnp.zeros((tq, ncp), F32)
        for z in range(HG_A):
            hcol = (g * HG_A + z) * HD
            s = lax.dot_general(q[:, hcol:hcol + HD], kc, NT, precision=HIGHEST, preferred_element_type=F32) * scale
            s = jnp.where(valid, s, NEG)
            e = jnp.where(valid, jnp.exp(s - jnp.max(s, axis=-1, keepdims=True)), 0.0)
            p = e / jnp.maximum(jnp.sum(e, axis=-1, keepdims=True), 1e-30)
            imp = imp + p
            gcol = g * HG_A + z
            o_ref[0, :, hcol:hcol + HD] = gates[:, gcol:gcol + 1] * _dot(p.astype(BF16), vc)
        imp_blk = jnp.dot(imp, mmap, precision=HIGHEST, preferred_element_type=F32)
        score = jnp.where(causal, jnp.where(forced, BIG, imp_blk), -BIG)
        picked = _topk_mask(score, min(N_SEL_TOP, n_sel), n_sel) & causal
        bm_ref[0, g] = picked.astype(F32)


def _cmp_attn(zq3, kvcmp, n_cmp, n_sel, t0, tq):
    b, t, _ = zq3.shape
    ncp = kvcmp.shape[3]
    nsp = -(-n_sel // LANES) * LANES
    cfg = dict(tq=tq, n_cmp=n_cmp, n_sel=n_sel, t0=t0)
    return pl.pallas_call(
        functools.partial(_cmp_attn_kernel, cfg), grid=(b, t // tq),
        in_specs=[pl.BlockSpec((1, tq, H_A * HD), lambda bi, i: (bi, i, QA0 // (H_A * HD))),
                  pl.BlockSpec((1, tq, LANES), lambda bi, i: (bi, i, GA0 // LANES)),
                  pl.BlockSpec((1, 2, G_A, ncp, HD), lambda bi, i: (bi, 0, 0, 0, 0))],
        out_specs=[pl.BlockSpec((1, tq, H_A * HD), lambda bi, i: (bi, i, 0)),
                   pl.BlockSpec((1, G_A, tq, nsp), lambda bi, i: (bi, 0, i, 0))],
        out_shape=[jax.ShapeDtypeStruct((b, t, H_A * HD), F32), jax.ShapeDtypeStruct((b, G_A, t, nsp), F32)],
        compiler_params=_cparams(2), name="nsa_cmp_attn")(zq3, zq3, kvcmp)


def _means_kernel(cfg, *refs):
    paged, n_kv = cfg["paged"], cfg["n_kv"]
    refs = list(refs)
    if paged:
        refs.pop(0)
    kv_refs, o_ref = refs[:n_kv], refs[n_kv]
    j = pl.program_id(1)

    @pl.when(j == 0)
    def _():
        o_ref[...] = jnp.zeros(o_ref.shape, F32)

    chunks_per_block = MOBA_BLOCK // LANES
    lane = lax.broadcasted_iota(I32, o_ref.shape[1:], 1)
    acc = o_ref[0]
    chunk0 = j * cfg["chunks_per_step"]
    for n, r in enumerate(kv_refs):
        x = r[0]
        for c in range(x.shape[1] // LANES):
            blk = (chunk0 + n * (x.shape[1] // LANES) + c) // chunks_per_block
            s = jnp.sum(x[:, c * LANES:(c + 1) * LANES], axis=1, keepdims=True)
            acc = acc + jnp.where(lane == blk, s, 0.0)
    o_ref[0] = acc

    @pl.when(j == pl.num_programs(1) - 1)
    def _():
        o_ref[0] = o_ref[0] / float(MOBA_BLOCK)


def _flash_kernel(cfg, *refs):
    kind, jobs, rows = cfg["kind"], cfg["jobs"], cfg["rows"]
    stacked, t0 = cfg["stacked"], cfg["t0"]
    it = iter(refs)
    if cfg["paged"]:
        next(it)
    q_ref = next(it)
    ga_ref = next(it) if kind in ("sel", "win") else None
    bm_ref = next(it) if kind == "sel" else None
    mt_ref = next(it) if kind == "moba" else None
    dp_ref = next(it) if kind == "diff" else None
    ng_ref = next(it) if kind == "diff" else None
    kv_refs = [next(it) for _ in range(cfg["n_kv"])]
    tail_ref = next(it) if cfg["tail"] else None
    o_ref = next(it)
    m_s, l_s, acc_s = next(it), next(it), next(it)
    bm_s = next(it) if kind == "moba" else None

    i, j = pl.program_id(1), pl.program_id(2)
    nk = pl.num_programs(2)
    rid = lax.broadcasted_iota(I32, (rows, 1), 0)
    qpos = t0 + (rid & (SUBLANES - 1)) if stacked else t0 + i * rows + rid
    q = q_ref[0]
    scale = cfg["scale"]

    @pl.when(j == 0)
    def _init():
        m_s[...] = jnp.full(m_s.shape, NEG, F32)
        l_s[...] = jnp.zeros(l_s.shape, F32)
        acc_s[...] = jnp.zeros(acc_s.shape, F32)
        if kind == "moba":
            lane = lax.broadcasted_iota(I32, (rows, LANES), 1)
            own = lax.shift_right_logical(qpos, 8)
            past = lane < own
            for mi, (qc0, qw, kr0, kw) in enumerate(cfg["mask_jobs"]):
                gate = jnp.dot(q[:, qc0:qc0 + qw], mt_ref[0, kr0:kr0 + kw, :], precision=HIGHEST,
                               preferred_element_type=F32)
                score = jnp.where(past, gate, -BIG)
                picked = _topk_mask(score, min(MOBA_TOPK, cfg["n_blk"]), cfg["n_blk"]) & past
                bm_s[mi] = (picked | (lane == own)).astype(F32)

    qb = q.astype(BF16)

    def process(kv, k0):
        tks = kv.shape[1]
        kpos = k0 + lax.broadcasted_iota(I32, (1, tks), 1)
        base = kpos <= qpos
        if kind == "win":
            base = base & (kpos >= qpos - WINDOW)
        masks = {}
        if kind in ("sel", "moba"):
            shift = 6 if kind == "sel" else 8
            n_mask = cfg["n_mask"]
            nbp = bm_ref.shape[3] if kind == "sel" else LANES
            kb = lax.shift_right_logical(k0 + lax.broadcasted_iota(I32, (nbp, tks), 1), shift)
            expand = (lax.broadcasted_iota(I32, (nbp, tks), 0) == kb).astype(BF16)
            for mi in range(n_mask):
                bm = bm_ref[0, mi] if kind == "sel" else bm_s[mi]
                masks[mi] = base & (_dot(bm.astype(BF16), expand) > 0.5)
        kvb = kv.astype(BF16)
        for ji, (qc0, qw, kr0, kw, vr0, vw, mi) in enumerate(jobs):
            valid = masks[mi] if masks else base
            s = _dot(qb[:, qc0:qc0 + qw], kvb[kr0:kr0 + kw]) * scale
            s = jnp.where(valid, s, NEG)
            m_prev = m_s[ji][:, 0:1]
            m_new = jnp.maximum(m_prev, jnp.max(s, axis=-1, keepdims=True))
            alpha = jnp.exp(m_prev - m_new)
            p = jnp.where(valid, jnp.exp(s - m_new), 0.0)
            l_s[ji] = jnp.broadcast_to(alpha * l_s[ji][:, 0:1] + jnp.sum(p, axis=-1, keepdims=True), (rows, LANES))
            m_s[ji] = jnp.broadcast_to(m_new, (rows, LANES))
            acc_s[ji, :, 0:vw] = alpha * acc_s[ji, :, 0:vw] + _dot_nt(p.astype(BF16), kvb[vr0:vr0 + vw])

    def normalized(ji, vw):
        return acc_s[ji, :, 0:vw] / jnp.maximum(l_s[ji][:, 0:1], 1e-30)

    def finalize():
        if kind in ("sel", "win"):
            gates = ga_ref[0]
            branch = 1 if kind == "sel" else 2
            for h in range(H_A):
                gcol = branch * H_A + h
                if stacked:
                    o = normalized(0, cfg["jobs"][0][5])[h * SUBLANES:(h + 1) * SUBLANES,
                                                         (h // HG_A) * HD:(h // HG_A + 1) * HD]
                else:
                    o = normalized(h, HD)
                o_ref[0, :, h * HD:(h + 1) * HD] = gates[:, gcol:gcol + 1] * o
        elif kind == "moba":
            for h in range(H_B):
                if stacked:
                    o = normalized(0, H_B * HD)[h * SUBLANES:(h + 1) * SUBLANES, h * HD:(h + 1) * HD]
                else:
                    o = normalized(h, HD)
                o_ref[0, :, h * HD:(h + 1) * HD] = o
        else:
            dp = dp_ref[...]
            lam_init = cfg["lam_init"]
            lam = (jnp.exp(jnp.sum(dp[0:1] * dp[1:2], axis=-1, keepdims=True))
                   - jnp.exp(jnp.sum(dp[2:3] * dp[3:4], axis=-1, keepdims=True)) + lam_init)
            ng = ng_ref[...]
            full = normalized(0, H_C * DV_C) if stacked else None
            for h in range(H_C):
                if stacked:
                    o1 = full[(2 * h) * SUBLANES:(2 * h + 1) * SUBLANES, h * DV_C:(h + 1) * DV_C]
                    o2 = full[(2 * h + 1) * SUBLANES:(2 * h + 2) * SUBLANES, h * DV_C:(h + 1) * DV_C]
                else:
                    o1, o2 = normalized(2 * h, DV_C), normalized(2 * h + 1, DV_C)
                o_ref[0, :, h * DV_C:(h + 1) * DV_C] = _rms(o1 - lam * o2, ng) * (1.0 - lam_init)

    if cfg["paged"]:
        for n, r in enumerate(kv_refs):
            process(r[0], (j * cfg["n_kv"] + n) * r.shape[2])

        @pl.when(j == nk - 1)
        def _last():
            if tail_ref is not None:
                process(tail_ref[0], cfg["tail_k0"])
            finalize()
    else:
        k0s = cfg["kv_k0"]
        if k0s is None:
            tk = kv_refs[0].shape[2]
            active = j * tk <= (t0 - cfg["k_base"]) + i * rows + rows - 1
            if kind == "win":
                active = active & ((j + 1) * tk - 1 >= (t0 - cfg["k_base"]) + i * rows - WINDOW)

            @pl.when(active)
            def _tile():
                process(kv_refs[0][0], cfg["k_base"] + j * tk)
        else:
            for r, k0 in zip(kv_refs, k0s):
                process(r[0], k0)

        @pl.when(j == nk - 1)
        def _last():
            finalize()


def _head_jobs(kind):
    if kind in ("sel", "win"):
        return [((g * HG_A + z) * HD, HD, g * HD, HD, G_A * HD + g * HD, HD, g) for g in range(G_A) for z in range(HG_A)]
    if kind == "moba":
        return [(h * HD, HD, h * HD, HD, H_B * HD + h * HD, HD, h) for h in range(H_B)]
    return [(h * DV_C + s * DH_C, DH_C, h * DV_C + s * DH_C, DH_C, H_C * DV_C + h * DV_C, DV_C, 0)
            for h in range(H_C) for s in range(2)]


def _stacked_jobs(kind):
    kc = {"sel": G_A * HD, "moba": H_B * HD, "diff": H_C * DV_C}[kind]
    return [(0, kc, 0, kc, kc, kc, 0)]


def _flash(kind, q, kvs, *, t0, out_w, stacked, q_block, gates=None, bm=None, means=None, diff_par=None,
           tile=None, kv_k0=None, k_base=0, paged=None, tail=None, n_blk=None, lam_init=None):
    b = q.shape[0]
    rows, qw, qblk = q_block
    nq = q.shape[1] // rows
    jobs = _stacked_jobs(kind) if stacked else _head_jobs(kind)
    vmax = max(jb[5] for jb in jobs)
    cfg = dict(kind=kind, jobs=jobs, rows=rows, stacked=stacked, t0=t0, paged=paged is not None,
               tail=tail is not None, kv_k0=kv_k0, k_base=k_base, n_blk=n_blk, lam_init=lam_init,
               scale=(DH_C if kind == "diff" else HD) ** -0.5, n_mask=1 if stacked else (G_A if kind == "sel" else H_B))
    if kind == "moba":
        cfg["mask_jobs"] = [(jb[0], jb[1], jb[2], jb[3]) for jb in jobs]
    operands, in_specs = [], []
    npre = 0
    if paged is not None:
        page_table, pool, row0, n_pages, per_step = paged
        npre = 1
        nk = n_pages // per_step
        cfg["n_kv"] = per_step
        cfg["tail_k0"] = n_pages * pool.shape[2]
    elif tile is not None:
        nk = kvs[0].shape[2] // tile
        cfg["n_kv"] = 1
    else:
        nk = 1
        cfg["n_kv"] = len(kvs)

    def add(x, block, imap):
        operands.append(x)
        in_specs.append(pl.BlockSpec(block, imap))

    out_rows = SUBLANES if stacked else rows
    add(q, (1, rows, qw), lambda bi, i, j, *_: (bi, i, qblk))
    if kind in ("sel", "win"):
        add(gates, (1, out_rows, LANES), lambda bi, i, j, *_: (bi, i, gates.shape[2] // LANES - 1))
    if kind == "sel":
        add(bm, (1, bm.shape[1], rows, bm.shape[3]), lambda bi, i, j, *_: (bi, 0, i, 0))
    if kind == "moba":
        add(means, (1,) + means.shape[1:], lambda bi, i, j, *_: (bi, 0, 0))
    if kind == "diff":
        add(diff_par[0], diff_par[0].shape, lambda bi, i, j, *_: (0, 0))
        add(diff_par[1], diff_par[1].shape, lambda bi, i, j, *_: (0, 0))
    if paged is not None:
        w, pg = pool.shape[1], pool.shape[2]
        for n in range(per_step):
            add(pool, (1, w, pg),
                lambda bi, i, j, pt, n=n: (row0 + pt[bi * n_pages + j * per_step + n], 0, 0))
    elif tile is not None:
        w = kvs[0].shape[1]

        def kv_map(bi, i, j, *_):
            hi = (t0 - k_base + i * rows + rows - 1) // tile
            lo = jnp.maximum(t0 - k_base + i * rows - WINDOW, 0) // tile if kind == "win" else 0
            return (bi, 0, jnp.clip(j, lo, hi))

        add(kvs[0], (1, w, tile), kv_map)
    else:
        for x in kvs:
            add(x, (1,) + x.shape[1:], lambda bi, i, j, *_: (bi, 0, 0))
    if tail is not None:
        add(tail, (1,) + tail.shape[1:], lambda bi, i, j, *_: (bi, 0, 0))
    scratch = [pltpu.VMEM((len(jobs), rows, LANES), F32), pltpu.VMEM((len(jobs), rows, LANES), F32),
               pltpu.VMEM((len(jobs), rows, vmax), F32)]
    if kind == "moba":
        scratch.append(pltpu.VMEM((cfg["n_mask"], rows, LANES), F32))
    grid_spec = pltpu.PrefetchScalarGridSpec(
        num_scalar_prefetch=npre, grid=(b, nq, nk), in_specs=in_specs,
        out_specs=pl.BlockSpec((1, out_rows, out_w), lambda bi, i, j, *_: (bi, i, 0)), scratch_shapes=scratch)
    out_t = SUBLANES if stacked else q.shape[1]
    call = pl.pallas_call(functools.partial(_flash_kernel, cfg), grid_spec=grid_spec,
                          out_shape=jax.ShapeDtypeStruct((b, out_t, out_w), F32),
                          compiler_params=_cparams(3), name="attn_" + kind + ("_dec" if stacked or kv_k0 else ""))
    return call(*(([paged[0]] if paged is not None else []) + operands))


def _means(kvs, paged=None):
    if paged is not None:
        page_table, pool, row0, n_pages, per_step = paged
        b = page_table.shape[0] // n_pages
        kw, pg = pool.shape[1] // 2, pool.shape[2]
        cfg = dict(paged=True, n_kv=per_step, chunks_per_step=per_step * pg // LANES)
        in_specs = [pl.BlockSpec((1, kw, pg), lambda bi, j, pt, n=n: (row0 + pt[bi * n_pages + j * per_step + n], 0, 0))
                    for n in range(per_step)]
        operands = [page_table] + [pool] * per_step
        grid, npre = (b, n_pages // per_step), 1
    else:
        x = kvs
        b, kw = x.shape[0], x.shape[1] // 2
        cfg = dict(paged=False, n_kv=1, chunks_per_step=x.shape[2] // LANES)
        in_specs = [pl.BlockSpec((1, kw, x.shape[2]), lambda bi, j: (bi, 0, 0))]
        operands = [x]
        grid, npre = (b, 1), 0
    grid_spec = pltpu.PrefetchScalarGridSpec(
        num_scalar_prefetch=npre, grid=grid, in_specs=in_specs,
        out_specs=pl.BlockSpec((1, kw, LANES), lambda bi, j, *_: (bi, 0, 0)))
    return pl.pallas_call(functools.partial(_means_kernel, cfg), grid_spec=grid_spec,
                          out_shape=jax.ShapeDtypeStruct((b, kw, LANES), F32),
                          compiler_params=_cparams(2), name="moba_means")(*operands)


def _page_copy_kernel(n_kv, *refs):
    kv_refs, o_ref = refs[1:1 + n_kv], refs[1 + n_kv]
    pg = kv_refs[0].shape[2]
    for n, r in enumerate(kv_refs):
        o_ref[0, :, n * pg:(n + 1) * pg] = r[0]


def _page_copy(page_table, pool, row0, n_pages, per_step):
    b = page_table.shape[0] // n_pages
    w, pg = pool.shape[1], pool.shape[2]
    in_specs = [pl.BlockSpec((1, w, pg), lambda bi, j, pt, n=n: (row0 + pt[bi * n_pages + j * per_step + n], 0, 0))
                for n in range(per_step)]
    grid_spec = pltpu.PrefetchScalarGridSpec(
        num_scalar_prefetch=1, grid=(b, n_pages // per_step), in_specs=in_specs,
        out_specs=pl.BlockSpec((1, w, per_step * pg), lambda bi, j, pt: (bi, 0, j)))
    return pl.pallas_call(functools.partial(_page_copy_kernel, per_step), grid_spec=grid_spec,
                          out_shape=jax.ShapeDtypeStruct((b, w, n_pages * pg), F32),
                          compiler_params=_cparams(2), name="page_copy")(page_table, *([pool] * per_step))


def _mix_proj_kernel(n_exp, *refs):
    it = iter(refs)
    h_ref, o1_ref, o2_ref, o3_ref, ob_ref, oc_ref, wout_ref, gffn_ref = (next(it) for _ in range(8))
    router_ref = next(it) if n_exp else None
    h2_ref, f_ref = next(it), next(it)
    w_ref = next(it) if n_exp else None
    na = H_A * HD
    oa = (o1_ref[...] + o2_ref[...] + o3_ref[...]).astype(BF16)
    h2 = (h_ref[...] + _dot(oa, wout_ref[0:na, :]) + _dot(ob_ref[...].astype(BF16), wout_ref[na:2 * na, :])
          + _dot(oc_ref[...].astype(BF16), wout_ref[2 * na:, :]))
    h2_ref[...] = h2
    f = _rms(h2, gffn_ref[...])
    f_ref[...] = f.astype(BF16)
    if n_exp:
        logits = jnp.dot(f, router_ref[...], precision=HIGHEST, preferred_element_type=F32)
        lane = lax.broadcasted_iota(I32, logits.shape, 1)
        logits = jnp.where(lane < n_exp, logits, -jnp.inf)
        v1 = jnp.max(logits, axis=-1, keepdims=True)
        i1 = jnp.min(jnp.where(logits == v1, lane, LANES), axis=-1, keepdims=True)
        rest = jnp.where(lane == i1, -jnp.inf, logits)
        v2 = jnp.max(rest, axis=-1, keepdims=True)
        i2 = jnp.min(jnp.where(rest == v2, lane, LANES), axis=-1, keepdims=True)
        e2 = jnp.exp(v2 - v1)
        w_ref[...] = jnp.where(lane == i1, 1.0 / (1.0 + e2), 0.0) + jnp.where(lane == i2, e2 / (1.0 + e2), 0.0)


def _mix_proj(h, o1, o2, o3, ob, oc, wout, gffn, router, tm):
    m, d = h.shape
    n_exp = 0 if router is None else router[1]
    row = lambda i: (i, 0)
    full = lambda i: (0, 0)
    operands = [h, o1, o2, o3, ob, oc, wout, gffn.reshape(1, d)]
    in_specs = [pl.BlockSpec((tm, d), row)] + [pl.BlockSpec((tm, x.shape[1]), row) for x in (o1, o2, o3, ob, oc)]
    in_specs += [pl.BlockSpec(wout.shape, full), pl.BlockSpec((1, d), full)]
    out_specs = [pl.BlockSpec((tm, d), row), pl.BlockSpec((tm, d), row)]
    out_shape = [jax.ShapeDtypeStruct((m, d), F32), jax.ShapeDtypeStruct((m, d), BF16)]
    if n_exp:
        operands.append(router[0])
        in_specs.append(pl.BlockSpec(router[0].shape, full))
        out_specs.append(pl.BlockSpec((tm, LANES), row))
        out_shape.append(jax.ShapeDtypeStruct((m, LANES), F32))
    return pl.pallas_call(functools.partial(_mix_proj_kernel, n_exp), grid=(m // tm,), in_specs=in_specs,
                          out_specs=out_specs, out_shape=out_shape, compiler_params=_cparams(1),
                          name="mix_proj")(*operands)


def _ffn_kernel(cfg, *refs):
    moe, final, n_exp = cfg["moe"], cfg["final"], cfg["n_exp"]
    it = iter(refs)
    h2_ref, f_ref = next(it), next(it)
    w_ref = next(it) if moe else None
    wg_ref, wu_ref, wd_ref, gple_ref, pgate_ref, pproj_ref, pemb_ref = (next(it) for _ in range(7))
    gfin_ref = next(it) if final else None
    out_ref = next(it)
    y_ref = next(it) if final else None
    acc_s = next(it)
    if moe:
        tot_s = next(it)
        e, j, nf = pl.program_id(1), pl.program_id(2), pl.num_programs(2)
    else:
        e, j, nf = 0, pl.program_id(1), pl.num_programs(1)

    if moe:
        @pl.when((j == 0) & (e == 0))
        def _zero_total():
            tot_s[...] = jnp.zeros(tot_s.shape, F32)

    @pl.when(j == 0)
    def _zero():
        acc_s[...] = jnp.zeros(acc_s.shape, F32)

    fb = f_ref[...]
    wg, wu, wd = (wg_ref[0], wu_ref[0], wd_ref[0]) if moe else (wg_ref[...], wu_ref[...], wd_ref[...])
    gg = _dot(fb, wg)
    act = (gg * _sigmoid(gg)) * _dot(fb, wu)
    acc_s[...] += _dot(act.astype(BF16), wd)

    if moe:
        @pl.when(j == nf - 1)
        def _mix():
            w = w_ref[...]
            lane = lax.broadcasted_iota(I32, w.shape, 1)
            w_e = jnp.sum(jnp.where(lane == e, w, 0.0), axis=-1, keepdims=True)
            tot_s[...] += w_e * acc_s[...]
    last = (j == nf - 1) if not moe else ((j == nf - 1) & (e == n_exp - 1))

    @pl.when(last)
    def _epilogue():
        hn = h2_ref[...] + (tot_s[...] if moe else acc_s[...])
        gate = _sigmoid(_dot(_rms(hn, gple_ref[...]).astype(BF16), pgate_ref[...]))
        h3 = hn + gate * _dot(pemb_ref[...].astype(BF16), pproj_ref[...])
        out_ref[...] = h3
        if final:
            y_ref[...] = _rms(h3, gfin_ref[...])


def _ffn(h2, f, w_route, ffn, gple, pgate, pproj, pemb, gfin, tm, tf):
    m, d = h2.shape
    moe = w_route is not None
    dff = ffn[0].shape[-1]
    final = gfin is not None
    n_exp = ffn[0].shape[0] if moe else 1
    cfg = dict(moe=moe, final=final, n_exp=n_exp)
    if moe:
        grid = (m // tm, n_exp, dff // tf)
        row = lambda i, e, j: (i, 0)
        full = lambda i, e, j: (0, 0)
        wcol = pl.BlockSpec((1, d, tf), lambda i, e, j: (e, 0, j))
        wrow = pl.BlockSpec((1, tf, d), lambda i, e, j: (e, j, 0))
    else:
        grid = (m // tm, dff // tf)
        row = lambda i, j: (i, 0)
        full = lambda i, j: (0, 0)
        wcol = pl.BlockSpec((d, tf), lambda i, j: (0, j))
        wrow = pl.BlockSpec((tf, d), lambda i, j: (j, 0))
    vec = pl.BlockSpec((1, d), full)
    operands = [h2, f]
    in_specs = [pl.BlockSpec((tm, d), row), pl.BlockSpec((tm, d), row)]
    if moe:
        operands.append(w_route)
        in_specs.append(pl.BlockSpec((tm, LANES), row))
    operands += [ffn[0], ffn[1], ffn[2], gple.reshape(1, d), pgate, pproj, pemb]
    in_specs += [wcol, wcol, wrow, vec, pl.BlockSpec(pgate.shape, full), pl.BlockSpec(pproj.shape, full),
                 pl.BlockSpec((tm, pemb.shape[1]), row)]
    out_specs = [pl.BlockSpec((tm, d), row)]
    out_shape = [jax.ShapeDtypeStruct((m, d), F32)]
    if final:
        operands.append(gfin.reshape(1, d))
        in_specs.append(vec)
        out_specs.append(pl.BlockSpec((tm, d), row))
        out_shape.append(jax.ShapeDtypeStruct((m, d), F32))
    scratch = [pltpu.VMEM((tm, d), F32)] + ([pltpu.VMEM((tm, d), F32)] if moe else [])
    res = pl.pallas_call(functools.partial(_ffn_kernel, cfg), grid=grid, in_specs=in_specs, out_specs=out_specs,
                         out_shape=out_shape, scratch_shapes=scratch, compiler_params=_cparams(len(grid)),
                         name="ffn_moe" if moe else "ffn_dense")(*operands)
    return (res[0], res[1]) if final else (res[0], None)


def _post(h, o1, o2, o3, ob, oc, wout, gffn, ffn, gple, pgate, pproj, pemb, gfin, moe, tm, tf):
    router = (ffn[3], ffn[0].shape[0]) if moe else None
    res = _mix_proj(h, o1, o2, o3, ob, oc, wout, gffn, router, tm)
    return _ffn(res[0], res[1], res[2] if moe else None, ffn, gple, pgate, pproj, pemb, gfin, tm, tf)


def _split_offsets():
    sizes = (H_A * HD, W_NSA, W_NSA, W_NSA, 3 * H_A, H_B * HD, H_B * HD, H_B * HD, H_C * DV_C, H_C * DV_C, H_C * DV_C)
    offs, s = [], 0
    for n in sizes:
        offs.append((s, s + n))
        s += n
    return offs


def _layer_weights(i, w_in, k_w1, k_w2, k_pe, v_w1, v_w2, v_pe):
    (qa, kvc, kvs, kvw, ga, qb, kb, vb, qc, kc, vc) = _split_offsets()
    w = w_in[i]
    sl = lambda r: w[:, r[0]:r[1]]
    wq = jnp.concatenate([sl(qa), sl(qb), sl(qc), jnp.pad(sl(ga), ((0, 0), (0, LANES - 3 * H_A)))], axis=1).astype(BF16)
    wkv = jnp.concatenate([sl(kvc), sl(kvs), sl(kvw), sl(kb), sl(vb), sl(kc), sl(vc)], axis=1).T.astype(BF16)
    kd = STRIDE * HD

    def cmp_w(w1, w2, pe):
        w1i = w1[i]
        w1cat = jnp.concatenate([w1i[:STRIDE].reshape(kd, CMP_HID), w1i[STRIDE:].reshape(kd, CMP_HID)], axis=1)
        pe8 = jnp.broadcast_to(pe[i].reshape(1, 2 * kd), (SUBLANES, 2 * kd))
        return w1cat.astype(BF16), w1i.reshape(2 * kd, CMP_HID).astype(BF16), pe8, w2[i].astype(BF16)

    ck, cv = cmp_w(k_w1, k_w2, k_pe), cmp_w(v_w1, v_w2, v_pe)
    cmp_weights = tuple(jnp.stack([a, b]) for a, b in zip(ck, cv))
    return wq, wkv, cmp_weights


def _halves(cmp_t, n_half):
    b = cmp_t.shape[0]
    x = cmp_t[:, :, :n_half * STRIDE].reshape(b, 2, G_A, HD, n_half, STRIDE)
    return x.transpose(0, 1, 2, 4, 5, 3).reshape(b, 2, G_A, n_half, STRIDE * HD)


def _from_feature_major(x_t, heads, hd):
    b, _, t = x_t.shape
    return x_t.reshape(b, 2, heads, hd, t).transpose(0, 4, 1, 2, 3)


def _pool_view(cache):
    d, n, pg, two, h, hd = cache.shape
    return cache.transpose(0, 1, 3, 4, 5, 2).reshape(d * n, two * h * hd, pg)


def _stack_rows(x, n_heads, width, col_of_head, k_width):
    b = x.shape[0]
    out = jnp.zeros((b, n_heads, SUBLANES, k_width), x.dtype)
    for h in range(n_heads):
        out = out.at[:, h, :, col_of_head(h):col_of_head(h) + width].set(x[:, :, h * width:(h + 1) * width])
    return out.reshape(b, n_heads * SUBLANES, k_width)


def kernel(x_prompt, x_sample, cache_nsa_cmp, cache_nsa_sel, cache_nsa_win, cache_moba, cache_diff, page_table, p_prompt, p_sample, g_attn, w_in, w_out, nsa_cmp_k_w1, nsa_cmp_k_w2, nsa_cmp_k_pe, nsa_cmp_v_w1, nsa_cmp_v_w2, nsa_cmp_v_pe, diff_lq1, diff_lk1, diff_lq2, diff_lk2, diff_norm_g, g_ffn, ffn_w_gate, ffn_w_up, ffn_w_down, moe_router, moe_w_gate, moe_w_up, moe_w_down, ple_gate, ple_proj, g_ple, g_final):
    depth = w_in.shape[0]
    bp, tp, d = x_prompt.shape
    bs, ts, _ = x_sample.shape
    n_pool, page = cache_nsa_cmp.shape[1], cache_nsa_cmp.shape[2]
    n_pages = page_table.shape[1]
    past_len = n_pages * page
    w_buf = cache_nsa_win.shape[2]
    assert ts <= SUBLANES and tp % 512 == 0 and n_pages % 16 == 0 and past_len % MOBA_BLOCK == 0
    pt_flat = page_table.reshape(-1).astype(I32)
    pools = {"cmp": _pool_view(cache_nsa_cmp), "sel": _pool_view(cache_nsa_sel),
             "moba": _pool_view(cache_moba), "diff": _pool_view(cache_diff)}
    win_cache_t = cache_nsa_win.transpose(0, 1, 3, 4, 5, 2).reshape(depth, bs, W_NSA, w_buf)
    per_step = 16

    tab_p = _rope_tables(jnp.arange(tp, dtype=I32))
    tab_s = _rope_tables(past_len + (jnp.arange(bs * ts, dtype=I32) % ts))

    hp = x_prompt.reshape(bp * tp, d)
    hs = x_sample.reshape(bs * ts, d)
    outs_p = {k: [] for k in ("cmp", "sel", "win", "moba", "diff")}
    outs_s = {k: [] for k in ("cmp", "sel", "win", "moba", "diff")}
    y_p = y_s = None
    n_dense = 0
    for i in range(depth):
        wq, wkv, cmpw = _layer_weights(i, w_in, nsa_cmp_k_w1, nsa_cmp_k_w2, nsa_cmp_k_pe,
                                       nsa_cmp_v_w1, nsa_cmp_v_w2, nsa_cmp_v_pe)
        lam_init = 0.8 - 0.6 * math.exp(-0.3 * i)
        dpar = jnp.zeros((SUBLANES, LANES), F32)
        for r, v in enumerate((diff_lq1, diff_lk1, diff_lq2, diff_lk2)):
            dpar = dpar.at[r, :DH_C].set(v[i].astype(F32))
        diff_par = (dpar, diff_norm_g[i].reshape(1, DV_C))
        moe = i % 2 == 1
        jf = i // 2
        if moe:
            router = jnp.pad(moe_router[jf].astype(F32), ((0, 0), (0, LANES - moe_router.shape[2])))
            ffn = (moe_w_gate[jf].astype(BF16), moe_w_up[jf].astype(BF16), moe_w_down[jf].astype(BF16), router)
        else:
            ffn = (ffn_w_gate[jf].astype(BF16), ffn_w_up[jf].astype(BF16), ffn_w_down[jf].astype(BF16))
        dff = ffn[0].shape[-1]
        tf = dff // 2 if (dff // 2) % LANES == 0 else dff
        wout = w_out[i].astype(BF16)
        pgate, pproj = ple_gate[i].astype(BF16), ple_proj[i].astype(BF16)
        gfin = g_final if i == depth - 1 else None

        zq, cmp_t, sel_t, win_t, moba_t, diff_t = _proj_in(hp, g_attn[i], wq, wkv, tab_p, bp, tp, 512)
        for k, v in zip(("cmp", "sel", "win", "moba", "diff"), (cmp_t, sel_t, win_t, moba_t, diff_t)):
            outs_p[k].append(v)
        zq3 = zq.reshape(bp, tp, NQ)
        n_cmp = (tp - L_CMP) // STRIDE + 1
        kvcmp = _compress(_halves(cmp_t, tp // STRIDE), *cmpw)
        tq = 512
        o_cmp, bm = _cmp_attn(zq3, kvcmp, n_cmp, tp // L_SEL, 0, tq)
        qa_blk, qb_blk, qc_blk = (tq, H_A * HD, 0), (tq, H_B * HD, 1), (tq, H_C * DV_C, QC0 // (H_C * DV_C))
        o_sel = _flash("sel", zq3, [sel_t], t0=0, out_w=H_A * HD, stacked=False, q_block=qa_blk, gates=zq3, bm=bm, tile=512)
        o_win = _flash("win", zq3, [win_t], t0=0, out_w=H_A * HD, stacked=False, q_block=qa_blk, gates=zq3, tile=512)
        means = _means(moba_t)
        o_b = _flash("moba", zq3, [moba_t], t0=0, out_w=H_B * HD, stacked=False, q_block=qb_blk, means=means,
                     tile=512, n_blk=tp // MOBA_BLOCK)
        o_c = _flash("diff", zq3, [diff_t], t0=0, out_w=H_C * DV_C, stacked=False, q_block=qc_blk, diff_par=diff_par,
                     tile=512, lam_init=lam_init)
        flat = lambda x: x.reshape(bp * tp, x.shape[-1])
        hp, y = _post(hp, flat(o_cmp), flat(o_sel), flat(o_win), flat(o_b), flat(o_c), wout, g_ffn[i], ffn,
                      g_ple[i], pgate, pproj, p_prompt[i].reshape(bp * tp, -1), gfin, moe, 512, tf)
        y_p = y if y is not None else y_p

        ms = bs * ts
        zq, cmp_n, sel_n, win_n, moba_n, diff_n = _proj_in(hs, g_attn[i], wq, wkv, tab_s, 1, ms, ms)
        news = dict(cmp=cmp_n, sel=sel_n, win=win_n, moba=moba_n, diff=diff_n)
        for k, v in news.items():
            outs_s[k].append(v)

        def tail_of(x):
            w = x.shape[1]
            return jnp.pad(x.reshape(w, bs, ts).transpose(1, 0, 2), ((0, 0), (0, 0), (0, LANES - ts)))

        zq8 = jnp.pad(zq.reshape(bs, ts, NQ), ((0, 0), (0, SUBLANES - ts), (0, 0)))
        l_tot = past_len + ts
        n_cmp = (l_tot - L_CMP) // STRIDE + 1
        assert (n_cmp + 1) * STRIDE <= past_len
        cmp_log = _page_copy(pt_flat, pools["cmp"], i * n_pool, n_pages, per_step)
        kvcmp = _compress(_halves(cmp_log, n_cmp + 1), *cmpw)
        n_sel = -(-l_tot // L_SEL)
        o_cmp, bm = _cmp_attn(zq8, kvcmp, n_cmp, n_sel, past_len, SUBLANES)
        q_sel = _stack_rows(zq8[:, :, QA0:QA0 + H_A * HD], H_A, HD, lambda h: (h // HG_A) * HD, G_A * HD)
        bm_st = jnp.repeat(bm, HG_A, axis=1).reshape(bs, 1, H_A * SUBLANES, bm.shape[3])
        ga8 = zq8
        o_sel = _flash("sel", q_sel, None, t0=past_len, out_w=H_A * HD, stacked=True,
                       q_block=(H_A * SUBLANES, G_A * HD, 0), gates=ga8, bm=bm_st,
                       paged=(pt_flat, pools["sel"], i * n_pool, n_pages, per_step), tail=tail_of(sel_n))
        qa8_blk = (SUBLANES, H_A * HD, 0)
        o_win = _flash("win", zq8, [win_cache_t[i], tail_of(win_n)], t0=past_len, out_w=H_A * HD, stacked=False,
                       q_block=qa8_blk, gates=ga8, kv_k0=[past_len - w_buf, past_len])
        means = _means(None, paged=(pt_flat, pools["moba"], i * n_pool, n_pages, per_step))
        q_moba = _stack_rows(zq8[:, :, QB0:QB0 + H_B * HD], H_B, HD, lambda h: h * HD, H_B * HD)
        o_b = _flash("moba", q_moba, None, t0=past_len, out_w=H_B * HD, stacked=True,
                     q_block=(H_B * SUBLANES, H_B * HD, 0), means=means,
                     paged=(pt_flat, pools["moba"], i * n_pool, n_pages, per_step), tail=tail_of(moba_n),
                     n_blk=-(-l_tot // MOBA_BLOCK))
        q_diff = _stack_rows(zq8[:, :, QC0:QC0 + H_C * DV_C], 2 * H_C, DH_C, lambda u: u * DH_C, H_C * DV_C)
        o_c = _flash("diff", q_diff, None, t0=past_len, out_w=H_C * DV_C, stacked=True,
                     q_block=(2 * H_C * SUBLANES, H_C * DV_C, 0), diff_par=diff_par,
                     paged=(pt_flat, pools["diff"], i * n_pool, n_pages, per_step), tail=tail_of(diff_n),
                     lam_init=lam_init)
        flat = lambda x: x[:, :ts].reshape(ms, x.shape[-1])
        hs, y = _post(hs, flat(o_cmp), flat(o_sel), flat(o_win), flat(o_b), flat(o_c), wout, g_ffn[i], ffn,
                      g_ple[i], pgate, pproj, p_sample[i].reshape(ms, -1), gfin, moe, ms, tf)
        y_s = y if y is not None else y_s

    heads = dict(cmp=(G_A, HD), sel=(G_A, HD), win=(G_A, HD), moba=(H_B, HD), diff=(H_C, DV_C))
    res_p, res_s = {}, {}
    for k, (nh, hd) in heads.items():
        res_p[k] = jnp.stack([_from_feature_major(x, nh, hd) for x in outs_p[k]], axis=0)
        res_s[k] = jnp.stack([x[0].T.reshape(bs, ts, 2, nh, hd) for x in outs_s[k]], axis=0)
    keep = min(WINDOW, tp)
    win_p = res_p["win"][:, :, tp - keep:]
    win_all = jnp.concatenate([cache_nsa_win, res_s["win"]], axis=2)
    win_s = win_all[:, :, win_all.shape[2] - min(WINDOW, win_all.shape[2]):]
    return (y_p.reshape(bp, tp, d), y_s.reshape(bs, ts, d), res_p["cmp"], res_p["sel"], win_p, res_p["moba"],
            res_p["diff"], res_s["cmp"], res_s["sel"], win_s, res_s["moba"], res_s["diff"])
```

```python
import functools
import math

import jax
import jax.numpy as jnp
from jax import lax
from jax.experimental import pallas as pl
from jax.experimental.pallas import tpu as pltpu

F32 = jnp.float32
BF16 = jnp.bfloat16
I32 = jnp.int32

HD = 64
H_A = 6
G_A = 2
HG_A = H_A // G_A
H_B = 6
H_C = 4
DH_C = 32
DV_C = 2 * DH_C
ROPE_THETA = 500000.0
L_CMP = 32
STRIDE = 16
L_SEL = 64
SEL_RATIO = L_SEL // STRIDE
N_SEL_TOP = 16
N_LOCAL = 2
WINDOW = 512
CMP_HID = 2 * HD
MOBA_BLOCK = 256
MOBA_TOPK = 3
TOP_K = 2
EPS = 1e-6
BIG = 1e9
NEG = -1e30
LOG2E = math.log2(math.e)

LANES = 128
SUBLANES = 8
VMEM_LIMIT_BYTES = 56 * 1024 * 1024

QA0, QB0, QC0, GA0 = 0, H_A * HD, (H_A + H_B) * HD, (H_A + H_B) * HD + H_C * DV_C
NQ = GA0 + LANES
W_NSA = 2 * G_A * HD
W_MOBA = 2 * H_B * HD
W_DIFF = 2 * H_C * DV_C
CMP0, SEL0, WIN0, MOBA0 = 0, W_NSA, 2 * W_NSA, 3 * W_NSA
DIFF0 = MOBA0 + W_MOBA
NKV = DIFF0 + W_DIFF

HIGHEST = lax.Precision.HIGHEST
NT = (((1,), (1,)), ((), ()))


def _cparams(n_grid):
    return pltpu.CompilerParams(dimension_semantics=("arbitrary",) * n_grid,
                                vmem_limit_bytes=VMEM_LIMIT_BYTES)


def _dot(a, b):
    return jnp.dot(a, b, preferred_element_type=F32)


def _dot_nt(a, b):
    return lax.dot_general(a, b, NT, preferred_element_type=F32)


def _sigmoid(x):
    return 1.0 / (1.0 + jnp.exp(-x))


def _rms(x, g):
    return (x * lax.rsqrt(jnp.mean(x * x, axis=-1, keepdims=True) + EPS)) * g


def _topk_mask(score, k, n_valid):
    lane = lax.broadcasted_iota(I32, score.shape, 1)
    rank = jnp.zeros(score.shape, I32)
    for i in range(n_valid):
        col = jnp.broadcast_to(score[:, i:i + 1], score.shape)
        beats = (col > score) | ((col == score) & (lane > i))
        rank = rank + beats.astype(I32)
    return (rank < k) & (lane < n_valid)


def _proj_in_kernel(x_ref, g_ref, wq_ref, wkv_ref, c64_ref, sa64_ref, sb64_ref, c32_ref, sa32_ref, sb32_ref,
                    ck_ref, sk_ref, ck32_ref, sk32_ref, zq_ref, cmp_ref, sel_ref, win_ref, moba_ref, diff_ref):
    ab = _rms(x_ref[...], g_ref[...]).astype(BF16)

    def rope_q(z, c, sa, sb, half):
        return z * c + pltpu.roll(z, half, 1) * sa + pltpu.roll(z, LANES - half, 1) * sb

    for c0, width, kind in ((QA0, H_A * HD, 64), (QB0, H_B * HD, 64), (QC0, H_C * DV_C, 32), (GA0, LANES, 0)):
        z = _dot(ab, wq_ref[:, c0:c0 + width])
        for k in range(width // LANES):
            zc = z[:, k * LANES:(k + 1) * LANES]
            if kind == 64:
                zc = rope_q(zc, c64_ref[...], sa64_ref[...], sb64_ref[...], 8)
            elif kind == 32:
                zc = rope_q(zc, c32_ref[...], sa32_ref[...], sb32_ref[...], 4)
            else:
                zc = _sigmoid(zc)
            zq_ref[:, c0 + k * LANES:c0 + (k + 1) * LANES] = zc

    ck, sk, ck32, sk32 = ck_ref[...], sk_ref[...], ck32_ref[...], sk32_ref[...]

    def rope_k64(zh):
        x1, x2 = zh[0:8], zh[8:16]
        return jnp.concatenate([x1 * ck - x2 * sk, x2 * ck + x1 * sk, zh[16:HD]], axis=0)

    def rope_k32(zh):
        x = zh[0:8]
        return jnp.concatenate([x * ck32 + pltpu.roll(x, 4, 0) * sk32, zh[8:DH_C]], axis=0)

    for r0, width, o_ref, unit, n_rot in ((CMP0, W_NSA, cmp_ref, HD, G_A), (SEL0, W_NSA, sel_ref, HD, G_A),
                                          (WIN0, W_NSA, win_ref, HD, G_A), (MOBA0, W_MOBA, moba_ref, HD, H_B),
                                          (DIFF0, W_DIFF, diff_ref, DH_C, 2 * H_C)):
        z = _dot_nt(wkv_ref[r0:r0 + width, :], ab)
        parts = []
        for u in range(n_rot):
            zh = z[u * unit:(u + 1) * unit]
            parts.append(rope_k64(zh) if unit == HD else rope_k32(zh))
        parts.append(z[n_rot * unit:width])
        o_ref[0] = jnp.concatenate(parts, axis=0)


def _rope_tables(pos):
    posf = pos.astype(F32)[:, None]

    def ang(rot):
        half = rot // 2
        inv = ROPE_THETA ** (-2.0 * jnp.arange(half, dtype=F32) / rot)
        a = posf * inv
        return jnp.cos(a), jnp.sin(a)

    def token_major(period, cos, sin):
        half = cos.shape[1]
        lane = jnp.arange(LANES) % period
        idx = jnp.where(lane < half, lane, lane - half) % half
        lo, hi = lane < half, (lane >= half) & (lane < 2 * half)
        c = jnp.where(lo | hi, cos[:, idx], 1.0)
        sa = jnp.where(hi, sin[:, idx], 0.0)
        sb = jnp.where(lo, -sin[:, idx], 0.0)
        return c, sa, sb

    cos64, sin64 = ang(HD // 4)
    cos32, sin32 = ang(DH_C // 4)
    t64 = token_major(HD, cos64, sin64)
    t32 = token_major(DH_C, cos32, sin32)
    ck, sk = cos64.T, sin64.T
    ck32 = jnp.concatenate([cos32, cos32], axis=1).T
    sk32 = jnp.concatenate([-sin32, sin32], axis=1).T
    return t64 + t32 + (ck, sk, ck32, sk32)


def _proj_in(h, g, wq, wkv, tables, n_seq, t_seq, tm):
    m, d = h.shape
    per = t_seq // tm
    row = lambda i: (i, 0)
    tab = lambda i: (i % per, 0)
    tabk = lambda i: (0, i % per)
    fm = lambda i: (i // per, 0, i % per)
    full = lambda i: (0, 0)
    in_specs = [pl.BlockSpec((tm, d), row), pl.BlockSpec((1, d), full),
                pl.BlockSpec((d, NQ), full), pl.BlockSpec((NKV, d), full)]
    in_specs += [pl.BlockSpec((tm, LANES), tab)] * 6 + [pl.BlockSpec((SUBLANES, tm), tabk)] * 4
    widths = (W_NSA, W_NSA, W_NSA, W_MOBA, W_DIFF)
    out_specs = [pl.BlockSpec((tm, NQ), row)] + [pl.BlockSpec((1, w, tm), fm) for w in widths]
    out_shape = [jax.ShapeDtypeStruct((m, NQ), F32)] + [jax.ShapeDtypeStruct((n_seq, w, t_seq), F32) for w in widths]
    return pl.pallas_call(_proj_in_kernel, grid=(m // tm,), in_specs=in_specs, out_specs=out_specs,
                          out_shape=out_shape, compiler_params=_cparams(1), name="proj_in")(
        h, g.reshape(1, d), wq, wkv, *tables)


def _half_proj_kernel(cfg, *refs):
    refs = list(refs)
    if cfg["paged"]:
        refs.pop(0)
    n_kv = cfg["n_kv"]
    kv_refs, w_ref, o_ref, rows_s = refs[:n_kv], refs[n_kv], refs[n_kv + 1], refs[n_kv + 2]
    chunks = [(r, c) for r in kv_refs for c in range(r.shape[2] // LANES)]
    n_half = len(chunks) * LANES // STRIDE
    for kv in range(2):
        for n, (r, c) in enumerate(chunks):
            rows_s[n * LANES:(n + 1) * LANES, :] = r[0, kv * G_A * HD:(kv + 1) * G_A * HD, c * LANES:(c + 1) * LANES].T
        acc = jnp.zeros((n_half, 2 * G_A * CMP_HID), F32)
        for s in range(STRIDE):
            acc = acc + _dot(rows_s[pl.ds(s, n_half, stride=STRIDE), :].astype(BF16), w_ref[kv, s])
        o_ref[0, kv] = acc


def _half_proj(kvs, wbd, paged=None):
    if paged is not None:
        page_table, pool, row0, n_pages, per_step = paged
        b = page_table.shape[0] // n_pages
        w, pg = pool.shape[1], pool.shape[2]
        in_specs = [pl.BlockSpec((1, w, pg), lambda bi, j, pt, n=n: (row0 + pt[bi * n_pages + j * per_step + n], 0, 0))
                    for n in range(per_step)]
        operands = [page_table] + [pool] * per_step
        n_kv, grid, npre, step_tokens, total = per_step, (b, n_pages // per_step), 1, per_step * pg, n_pages * pg
    else:
        x = kvs
        b, w, total = x.shape
        in_specs = [pl.BlockSpec((1, w, total), lambda bi, j: (bi, 0, 0))]
        operands = [x]
        n_kv, grid, npre, step_tokens = 1, (b, 1), 0, total
    in_specs.append(pl.BlockSpec(wbd.shape, lambda bi, j, *_: (0, 0, 0, 0)))
    wo = 2 * G_A * CMP_HID
    grid_spec = pltpu.PrefetchScalarGridSpec(
        num_scalar_prefetch=npre, grid=grid, in_specs=in_specs,
        out_specs=pl.BlockSpec((1, 2, step_tokens // STRIDE, wo), lambda bi, j, *_: (bi, 0, j, 0)),
        scratch_shapes=[pltpu.VMEM((step_tokens, G_A * HD), F32)])
    return pl.pallas_call(functools.partial(_half_proj_kernel, dict(paged=paged is not None, n_kv=n_kv)),
                          grid_spec=grid_spec, out_shape=jax.ShapeDtypeStruct((b, 2, total // STRIDE, wo), F32),
                          compiler_params=_cparams(2), name="nsa_half_proj")(*operands, wbd)


def _compress_tail_kernel(a_ref, w1f_ref, pe_ref, w2_ref, o_ref):
    nh = a_ref.shape[2]
    a = a_ref[0, 0]
    pe_term = _dot(pe_ref[0].astype(BF16), w1f_ref[0])[0:1]
    hid = a[:, :CMP_HID] + pltpu.roll(a[:, CMP_HID:], nh - 1, 0) + pe_term
    act = hid * _sigmoid(hid)
    o_ref[0, 0, 0] = _dot(act.astype(BF16), w2_ref[0])


def _compress_tail(a, w1flat, pe8, w2):
    b, _, nh, _ = a.shape
    kd = STRIDE * HD
    return pl.pallas_call(
        _compress_tail_kernel, grid=(b, 2, G_A),
        in_specs=[pl.BlockSpec((1, 1, nh, 2 * CMP_HID), lambda bi, kv, gi: (bi, kv, 0, gi)),
                  pl.BlockSpec((1, 2 * kd, CMP_HID), lambda bi, kv, gi: (kv, 0, 0)),
                  pl.BlockSpec((1, SUBLANES, 2 * kd), lambda bi, kv, gi: (kv, 0, 0)),
                  pl.BlockSpec((1, CMP_HID, HD), lambda bi, kv, gi: (kv, 0, 0))],
        out_specs=pl.BlockSpec((1, 1, 1, nh, HD), lambda bi, kv, gi: (bi, kv, gi, 0, 0)),
        out_shape=jax.ShapeDtypeStruct((b, 2, G_A, nh, HD), F32),
        compiler_params=_cparams(3), name="nsa_compress")(a, w1flat, pe8, w2)


def _cmp_attn_kernel(cfg, q_ref, ga_ref, kv_ref, o_ref, bm_ref):
    tq, n_cmp, n_sel, t0 = cfg["tq"], cfg["n_cmp"], cfg["n_sel"], cfg["t0"]
    ncp, nsp = kv_ref.shape[3], bm_ref.shape[3]
    i = pl.program_id(1)
    qpos = t0 + i * tq + lax.broadcasted_iota(I32, (tq, 1), 0)
    q = q_ref[0]
    gates = ga_ref[0]
    n_idx = lax.broadcasted_iota(I32, (1, ncp), 1)
    valid = ((n_idx * STRIDE + (L_CMP - 1)) <= qpos) & (n_idx < n_cmp)
    nn = lax.broadcasted_iota(I32, (ncp, nsp), 0)
    jj = lax.broadcasted_iota(I32, (ncp, nsp), 1)
    blk = lax.shift_right_logical(nn, 2)
    overlap = ((blk == jj) | (((nn & 3) == 3) & (blk + 1 == jj))) & (nn < n_cmp)
    mmap = overlap.astype(F32)
    lane = lax.broadcasted_iota(I32, (tq, nsp), 1)
    qblk = lax.shift_right_logical(qpos, 6)
    causal = lane <= qblk
    forced = (lane == 0) | (lane > qblk - N_LOCAL)
    scale = HD ** -0.5
    for g in range(G_A):
        kc = kv_ref[0, 0, g]
        vc = kv_ref[0, 1, g].astype(BF16)
        imp = jnp.zeros((tq, ncp), F32)
        for z in range(HG_A):
            hcol = (g * HG_A + z) * HD
            s = lax.dot_general(q[:, hcol:hcol + HD], kc, NT, precision=HIGHEST, preferred_element_type=F32) * scale
            s = jnp.where(valid, s, NEG)
            e = jnp.where(valid, jnp.exp(s - jnp.max(s, axis=-1, keepdims=True)), 0.0)
            p = e / jnp.maximum(jnp.sum(e, axis=-1, keepdims=True), 1e-30)
            imp = imp + p
            gcol = g * HG_A + z
            o_ref[0, :, hcol:hcol + HD] = gates[:, gcol:gcol + 1] * _dot(p.astype(BF16), vc)
        imp_blk = jnp.dot(imp, mmap, precision=HIGHEST, preferred_element_type=F32)
        score = jnp.where(causal, jnp.where(forced, BIG, imp_blk), -BIG)
        picked = _topk_mask(score, min(N_SEL_TOP, n_sel), n_sel) & causal
        bm_ref[0, g] = picked.astype(F32)


def _cmp_attn(zq3, kvcmp, n_cmp, n_sel, t0, tq):
    b, t, _ = zq3.shape
    ncp = kvcmp.shape[3]
    nsp = -(-n_sel // LANES) * LANES
    cfg = dict(tq=tq, n_cmp=n_cmp, n_sel=n_sel, t0=t0)
    return pl.pallas_call(
        functools.partial(_cmp_attn_kernel, cfg), grid=(b, t // tq),
        in_specs=[pl.BlockSpec((1, tq, H_A * HD), lambda bi, i: (bi, i, QA0 // (H_A * HD))),
                  pl.BlockSpec((1, tq, LANES), lambda bi, i: (bi, i, GA0 // LANES)),
                  pl.BlockSpec((1, 2, G_A, ncp, HD), lambda bi, i: (bi, 0, 0, 0, 0))],
        out_specs=[pl.BlockSpec((1, tq, H_A * HD), lambda bi, i: (bi, i, 0)),
                   pl.BlockSpec((1, G_A, tq, nsp), lambda bi, i: (bi, 0, i, 0))],
        out_shape=[jax.ShapeDtypeStruct((b, t, H_A * HD), F32), jax.ShapeDtypeStruct((b, G_A, t, nsp), F32)],
        compiler_params=_cparams(2), name="nsa_cmp_attn")(zq3, zq3, kvcmp)


def _means_kernel(cfg, *refs):
    paged, n_kv = cfg["paged"], cfg["n_kv"]
    refs = list(refs)
    if paged:
        refs.pop(0)
    kv_refs, o_ref, acc_s = refs[:n_kv], refs[n_kv], refs[n_kv + 1]
    j = pl.program_id(1)

    @pl.when(j == 0)
    def _():
        acc_s[...] = jnp.zeros(acc_s.shape, F32)

    chunks_per_block = MOBA_BLOCK // LANES
    lane = lax.broadcasted_iota(I32, acc_s.shape, 1)
    acc = acc_s[...]
    chunk0 = j * cfg["chunks_per_step"]
    for n, r in enumerate(kv_refs):
        x = r[0]
        for c in range(x.shape[1] // LANES):
            blk = (chunk0 + n * (x.shape[1] // LANES) + c) // chunks_per_block
            s = jnp.sum(x[:, c * LANES:(c + 1) * LANES], axis=1, keepdims=True)
            acc = acc + jnp.where(lane == blk, s, 0.0)
    acc_s[...] = acc

    @pl.when(j == pl.num_programs(1) - 1)
    def _():
        o_ref[0] = (acc_s[...] / float(MOBA_BLOCK)).T


def _flash_kernel(cfg, *refs):
    kind, jobs, rows = cfg["kind"], cfg["jobs"], cfg["rows"]
    stacked, t0 = cfg["stacked"], cfg["t0"]
    it = iter(refs)
    if cfg["paged"]:
        next(it)
    q_ref = next(it)
    ga_ref = next(it) if kind in ("sel", "win") else None
    bm_ref = next(it) if kind == "sel" else None
    mt_ref = next(it) if kind == "moba" else None
    dp_ref = next(it) if kind == "diff" else None
    ng_ref = next(it) if kind == "diff" else None
    kv_refs = [next(it) for _ in range(cfg["n_kv"])]
    tail_ref = next(it) if cfg["tail"] else None
    tail_bm_ref = next(it) if (kind == "sel" and cfg["tail"]) else None
    o_ref = next(it)
    m_s, l_s, acc_s = next(it), next(it), next(it)
    bm_s = next(it) if kind == "moba" else None
    kvb_s = next(it) if cfg["chunked"] else None

    i, j = pl.program_id(1), pl.program_id(2)
    nk = pl.num_programs(2)
    qscale = cfg["scale"] * LOG2E
    n_blk = cfg["n_blk"]

    def positions(r0, n, axis):
        rid = r0 + lax.broadcasted_iota(I32, (n, 1) if axis == 0 else (1, n), axis)
        return t0 + (rid & (SUBLANES - 1)) if stacked else t0 + i * rows + rid

    @pl.when(j == 0)
    def _init():
        m_s[...] = jnp.full(m_s.shape, NEG, F32)
        l_s[...] = jnp.zeros(l_s.shape, F32)
        acc_s[...] = jnp.zeros(acc_s.shape, F32)
        if kind == "moba":
            q = q_ref[0]
            k_top = min(MOBA_TOPK, n_blk)
            if stacked:
                lane = lax.broadcasted_iota(I32, (rows, LANES), 1)
                own = lax.shift_right_logical(positions(0, rows, 0), 8)
                past = lane < own
                gate = lax.dot_general(q, mt_ref[0], NT, precision=HIGHEST, preferred_element_type=F32)
                picked = _topk_mask(jnp.where(past, gate, -BIG), k_top, n_blk) & past
                bm_s[0] = (picked | (lane == own)).astype(F32)
            else:
                nb8 = -(-n_blk // SUBLANES) * SUBLANES
                blk = lax.broadcasted_iota(I32, (nb8, rows), 0)
                own = lax.shift_right_logical(positions(0, rows, 1), 8)
                past = blk < own
                for mi, (qc0, qw, kr0, kw) in enumerate(cfg["mask_jobs"]):
                    gate = lax.dot_general(mt_ref[0, 0:nb8, kr0:kr0 + kw], q[:, qc0:qc0 + qw], NT,
                                           precision=HIGHEST, preferred_element_type=F32)
                    score = jnp.where(past, gate, -BIG)
                    rank = jnp.zeros((nb8, rows), I32)
                    for ib in range(n_blk):
                        row = score[ib:ib + 1, :]
                        rank = rank + ((row > score) | ((row == score) & (blk > ib))).astype(I32)
                    picked = ((rank < k_top) & past) | (blk == own)
                    full = jnp.concatenate([picked.astype(F32), jnp.zeros((LANES - nb8, rows), F32)], axis=0)
                    bm_s[mi] = full.T

    def update(ji, rsl, qj, tiles, vw):
        ss = []
        for kt, _, bias in tiles:
            s = _dot(qj, kt)
            ss.append(s if bias is None else s + bias)
        m_cur = lane_fold(ss, jnp.maximum)
        m_cur = jnp.broadcast_to(jnp.max(m_cur, axis=-1, keepdims=True), m_cur.shape)
        m_prev = m_s[ji, rsl, :]
        m_new = jnp.maximum(m_prev, m_cur)
        alpha = jnp.exp2(m_prev - m_new)
        ps = [jnp.exp2(s - lanes_to(m_new, s.shape[1])) for s in ss]
        tot = lane_fold(ps, jnp.add)
        tot = jnp.broadcast_to(jnp.sum(tot, axis=-1, keepdims=True), tot.shape)
        l_s[ji, rsl, :] = alpha * l_s[ji, rsl, :] + tot
        m_s[ji, rsl, :] = m_new
        pv = _dot_nt(ps[0].astype(BF16), tiles[0][1])
        for p, (_, vt, _) in zip(ps[1:], tiles[1:]):
            pv = pv + _dot_nt(p.astype(BF16), vt)
        acc_s[ji, rsl, 0:vw] = lanes_to(alpha, vw) * acc_s[ji, rsl, 0:vw] + pv

    def lane_fold(xs, op):
        out = None
        for x in xs:
            for c in range(x.shape[1] // LANES):
                part = x[:, c * LANES:(c + 1) * LANES]
                out = part if out is None else op(out, part)
        return out

    def lanes_to(x, width):
        if width <= LANES:
            return x[:, 0:width]
        reps = -(-width // LANES)
        return pltpu.repeat(x, reps, axis=1)[:, 0:width]

    def causal_bias(qp, k0, tks):
        kpos = k0 + lax.broadcasted_iota(I32, (1, tks), 1)
        ok = kpos <= qp
        if kind == "win":
            ok = ok & (kpos >= qp - WINDOW)
        return jnp.where(ok, 0.0, NEG)

    def block_bias(bm_rows, expand):
        return (_dot(bm_rows.astype(BF16), expand) - 1.0) * (-NEG)

    def chunked_tile(k0):
        tk = kvb_s.shape[1]
        kvb_s[...] = kv_refs[0][0].astype(BF16)
        expand = None
        if kind in ("sel", "moba"):
            shift = 6 if kind == "sel" else 8
            kb = lax.shift_right_logical(k0 + lax.broadcasted_iota(I32, (LANES, tk), 1), shift)
            expand = (lax.broadcasted_iota(I32, (LANES, tk), 0) == kb).astype(BF16)

        n_chunk = rows // cfg["row_chunk"]

        def chunk(c, carry):
            if n_chunk == 1:
                r0, rsl = 0, slice(None)
            else:
                r0 = pl.multiple_of(c * cfg["row_chunk"], cfg["row_chunk"])
                rsl = pl.ds(r0, cfg["row_chunk"])
            qc = (q_ref[0, rsl, :] * qscale).astype(BF16)
            base = causal_bias(positions(r0, cfg["row_chunk"], 0), k0, tk)
            biases = {}
            for ji, (qc0, qw, kr0, kw, vr0, vw, mi) in enumerate(jobs):
                if mi not in biases:
                    if kind == "sel":
                        biases[mi] = base + block_bias(bm_ref[0, mi, rsl, :], expand)
                    elif kind == "moba":
                        biases[mi] = base + block_bias(bm_s[mi, rsl, :], expand)
                    else:
                        biases[mi] = base
                update(ji, rsl, qc[:, qc0:qc0 + qw],
                       [(kvb_s[kr0:kr0 + kw, :], kvb_s[vr0:vr0 + vw, :], biases[mi])], vw)
            return carry

        if n_chunk == 1:
            chunk(0, 0)
        else:
            lax.fori_loop(0, n_chunk, chunk, 0)

    def column_bias(col):
        return (col - 1.0) * (-NEG)

    def whole_tiles(pairs, page_cols=None, tail_col=None):
        qb = (q_ref[0] * qscale).astype(BF16)
        all_rows = slice(None)
        qp = positions(0, rows, 0)
        kvbs = [kv.astype(BF16) for kv, _ in pairs]
        for ji, (qc0, qw, kr0, kw, vr0, vw, mi) in enumerate(jobs):
            tiles = []
            for n, ((kv, k0), kvb) in enumerate(zip(pairs, kvbs)):
                bias = None if k0 is None else causal_bias(qp, k0, kv.shape[1])
                extra = page_cols[n] if page_cols is not None else tail_col
                if extra is not None:
                    bias = extra if bias is None else bias + extra
                tiles.append((kvb[kr0:kr0 + kw], kvb[vr0:vr0 + vw], bias))
            update(ji, all_rows, qb[:, qc0:qc0 + qw], tiles, vw)

    def normalized(ji, vw):
        return acc_s[ji, :, 0:vw] / jnp.maximum(lanes_to(l_s[ji], vw), 1e-30)

    def finalize():
        if kind in ("sel", "win"):
            gates = ga_ref[0]
            branch = 1 if kind == "sel" else 2
            for h in range(H_A):
                gcol = branch * H_A + h
                if stacked:
                    o = normalized(0, cfg["jobs"][0][5])[h * SUBLANES:(h + 1) * SUBLANES,
                                                         (h // HG_A) * HD:(h // HG_A + 1) * HD]
                else:
                    o = normalized(h, HD)
                o_ref[0, :, h * HD:(h + 1) * HD] = gates[:, gcol:gcol + 1] * o
        elif kind == "moba":
            for h in range(H_B):
                if stacked:
                    o = normalized(0, H_B * HD)[h * SUBLANES:(h + 1) * SUBLANES, h * HD:(h + 1) * HD]
                else:
                    o = normalized(h, HD)
                o_ref[0, :, h * HD:(h + 1) * HD] = o
        else:
            dp = dp_ref[...]
            lam_init = cfg["lam_init"]
            lam = (jnp.exp(jnp.sum(dp[0:1] * dp[1:2], axis=-1, keepdims=True))
                   - jnp.exp(jnp.sum(dp[2:3] * dp[3:4], axis=-1, keepdims=True)) + lam_init)
            ng = ng_ref[...]
            full = normalized(0, H_C * DV_C) if stacked else None
            for h in range(H_C):
                if stacked:
                    o1 = full[(2 * h) * SUBLANES:(2 * h + 1) * SUBLANES, h * DV_C:(h + 1) * DV_C]
                    o2 = full[(2 * h + 1) * SUBLANES:(2 * h + 2) * SUBLANES, h * DV_C:(h + 1) * DV_C]
                else:
                    o1, o2 = normalized(2 * h, DV_C), normalized(2 * h + 1, DV_C)
                o_ref[0, :, h * DV_C:(h + 1) * DV_C] = _rms(o1 - lam * o2, ng) * (1.0 - lam_init)

    if cfg["paged"]:
        n_kv = cfg["n_kv"]
        page_cols = None
        if kind == "sel":
            bm = bm_ref[0, 0]
            half = lax.broadcasted_iota(I32, (rows, LANES), 1) < L_SEL
            page_cols = [column_bias(jnp.where(half, bm[:, 2 * n:2 * n + 1], bm[:, 2 * n + 1:2 * n + 2]))
                         for n in range(n_kv)]
        elif kind == "moba":
            per_blk = MOBA_BLOCK // kv_refs[0].shape[2]
            bm = pltpu.roll(bm_s[0], (LANES - (n_kv // per_blk) * j) % LANES, 1)
            page_cols = [column_bias(bm[:, n // per_blk:n // per_blk + 1]) for n in range(n_kv)]
        whole_tiles([(r[0], None) for r in kv_refs], page_cols=page_cols)

        @pl.when(j == nk - 1)
        def _last():
            if tail_ref is not None:
                tail_col = None
                if kind == "sel":
                    tail_col = column_bias(tail_bm_ref[0, 0][:, 0:1])
                elif kind == "moba":
                    tail_col = column_bias(bm_s[0][:, n_blk - 1:n_blk])
                whole_tiles([(tail_ref[0], cfg["tail_k0"])], tail_col=tail_col)
            finalize()
    else:
        k0s = cfg["kv_k0"]
        if k0s is None:
            tk = kv_refs[0].shape[2]
            active = j * tk <= (t0 - cfg["k_base"]) + i * rows + rows - 1
            if kind == "win":
                active = active & ((j + 1) * tk - 1 >= (t0 - cfg["k_base"]) + i * rows - WINDOW)

            @pl.when(active)
            def _tile():
                chunked_tile(cfg["k_base"] + j * tk)
        else:
            whole_tiles([(r[0], k0) for r, k0 in zip(kv_refs, k0s)])

        @pl.when(j == nk - 1)
        def _last():
            finalize()


def _head_jobs(kind):
    if kind in ("sel", "win"):
        return [((g * HG_A + z) * HD, HD, g * HD, HD, G_A * HD + g * HD, HD, g) for g in range(G_A) for z in range(HG_A)]
    if kind == "moba":
        return [(h * HD, HD, h * HD, HD, H_B * HD + h * HD, HD, h) for h in range(H_B)]
    return [(h * DV_C + s * DH_C, DH_C, h * DV_C + s * DH_C, DH_C, H_C * DV_C + h * DV_C, DV_C, 0)
            for h in range(H_C) for s in range(2)]


def _stacked_jobs(kind):
    kc = {"sel": G_A * HD, "moba": H_B * HD, "diff": H_C * DV_C}[kind]
    return [(0, kc, 0, kc, kc, kc, 0)]


def _flash(kind, q, kvs, *, t0, out_w, stacked, q_block, gates=None, bm=None, means=None, diff_par=None,
           tile=None, kv_k0=None, k_base=0, paged=None, tail=None, n_blk=None, lam_init=None, row_chunk=None):
    b = q.shape[0]
    rows, qw, qblk = q_block
    nq = q.shape[1] // rows
    jobs = _stacked_jobs(kind) if stacked else _head_jobs(kind)
    vmax = max(jb[5] for jb in jobs)
    cfg = dict(kind=kind, jobs=jobs, rows=rows, stacked=stacked, t0=t0, paged=paged is not None,
               tail=tail is not None, kv_k0=kv_k0, k_base=k_base, n_blk=n_blk, lam_init=lam_init,
               chunked=tile is not None, row_chunk=row_chunk or rows,
               scale=(DH_C if kind == "diff" else HD) ** -0.5, n_mask=1 if stacked else (G_A if kind == "sel" else H_B))
    if kind == "moba":
        cfg["mask_jobs"] = [(jb[0], jb[1], jb[2], jb[3]) for jb in jobs]
    operands, in_specs = [], []
    npre = 0
    if paged is not None:
        page_table, pool, row0, n_pages, per_step = paged
        npre = 1
        nk = n_pages // per_step
        cfg["n_kv"] = per_step
        cfg["tail_k0"] = n_pages * pool.shape[2]
    elif tile is not None:
        nk = kvs[0].shape[2] // tile
        cfg["n_kv"] = 1
    else:
        nk = 1
        cfg["n_kv"] = len(kvs)

    def add(x, block, imap):
        operands.append(x)
        in_specs.append(pl.BlockSpec(block, imap))

    out_rows = SUBLANES if stacked else rows
    add(q, (1, rows, qw), lambda bi, i, j, *_: (bi, i, qblk))
    if kind in ("sel", "win"):
        add(gates, (1, out_rows, LANES), lambda bi, i, j, *_: (bi, i, gates.shape[2] // LANES - 1))
    if kind == "sel" and paged is not None:
        add(bm, (1, 1, rows, LANES), lambda bi, i, j, *_: (bi, j, 0, 0))
    elif kind == "sel":
        add(bm, (1, bm.shape[1], rows, bm.shape[3]), lambda bi, i, j, *_: (bi, 0, i, 0))
    if kind == "moba":
        add(means, (1,) + means.shape[1:], lambda bi, i, j, *_: (bi, 0, 0))
    if kind == "diff":
        add(diff_par[0], diff_par[0].shape, lambda bi, i, j, *_: (0, 0))
        add(diff_par[1], diff_par[1].shape, lambda bi, i, j, *_: (0, 0))
    if paged is not None:
        w, pg = pool.shape[1], pool.shape[2]
        for n in range(per_step):
            add(pool, (1, w, pg),
                lambda bi, i, j, pt, n=n: (row0 + pt[bi * n_pages + j * per_step + n], 0, 0))
    elif tile is not None:
        w = kvs[0].shape[1]

        def kv_map(bi, i, j, *_):
            hi = (t0 - k_base + i * rows + rows - 1) // tile
            lo = jnp.maximum(t0 - k_base + i * rows - WINDOW, 0) // tile if kind == "win" else 0
            return (bi, 0, jnp.clip(j, lo, hi))

        add(kvs[0], (1, w, tile), kv_map)
    else:
        for x in kvs:
            add(x, (1,) + x.shape[1:], lambda bi, i, j, *_: (bi, 0, 0))
    if tail is not None:
        add(tail, (1,) + tail.shape[1:], lambda bi, i, j, *_: (bi, 0, 0))
        if kind == "sel":
            add(bm, (1, 1, rows, LANES), lambda bi, i, j, *_: (bi, nk, 0, 0))
    scratch = [pltpu.VMEM((len(jobs), rows, LANES), F32), pltpu.VMEM((len(jobs), rows, LANES), F32),
               pltpu.VMEM((len(jobs), rows, vmax), F32)]
    if kind == "moba":
        scratch.append(pltpu.VMEM((cfg["n_mask"], rows, LANES), F32))
    if tile is not None:
        scratch.append(pltpu.VMEM((kvs[0].shape[1], tile), BF16))
    grid_spec = pltpu.PrefetchScalarGridSpec(
        num_scalar_prefetch=npre, grid=(b, nq, nk), in_specs=in_specs,
        out_specs=pl.BlockSpec((1, out_rows, out_w), lambda bi, i, j, *_: (bi, i, 0)), scratch_shapes=scratch)
    out_t = SUBLANES if stacked else q.shape[1]
    call = pl.pallas_call(functools.partial(_flash_kernel, cfg), grid_spec=grid_spec,
                          out_shape=jax.ShapeDtypeStruct((b, out_t, out_w), F32),
                          compiler_params=_cparams(3), name="attn_" + kind + ("_dec" if stacked or kv_k0 else ""))
    return call(*(([paged[0]] if paged is not None else []) + operands))


def _means(kvs, paged=None):
    if paged is not None:
        page_table, pool, row0, n_pages, per_step = paged
        b = page_table.shape[0] // n_pages
        kw, pg = pool.shape[1] // 2, pool.shape[2]
        cfg = dict(paged=True, n_kv=per_step, chunks_per_step=per_step * pg // LANES)
        in_specs = [pl.BlockSpec((1, kw, pg), lambda bi, j, pt, n=n: (row0 + pt[bi * n_pages + j * per_step + n], 0, 0))
                    for n in range(per_step)]
        operands = [page_table] + [pool] * per_step
        grid, npre = (b, n_pages // per_step), 1
    else:
        x = kvs
        b, kw = x.shape[0], x.shape[1] // 2
        cfg = dict(paged=False, n_kv=1, chunks_per_step=x.shape[2] // LANES)
        in_specs = [pl.BlockSpec((1, kw, x.shape[2]), lambda bi, j: (bi, 0, 0))]
        operands = [x]
        grid, npre = (b, 1), 0
    grid_spec = pltpu.PrefetchScalarGridSpec(
        num_scalar_prefetch=npre, grid=grid, in_specs=in_specs,
        out_specs=pl.BlockSpec((1, LANES, kw), lambda bi, j, *_: (bi, 0, 0)),
        scratch_shapes=[pltpu.VMEM((kw, LANES), F32)])
    return pl.pallas_call(functools.partial(_means_kernel, cfg), grid_spec=grid_spec,
                          out_shape=jax.ShapeDtypeStruct((b, LANES, kw), F32),
                          compiler_params=_cparams(2), name="moba_means")(*operands)


def _mix_proj_kernel(n_exp, *refs):
    it = iter(refs)
    h_ref, o1_ref, o2_ref, o3_ref, ob_ref, oc_ref, wout_ref, gffn_ref = (next(it) for _ in range(8))
    router_ref = next(it) if n_exp else None
    h2_ref, f_ref = next(it), next(it)
    w_ref = next(it) if n_exp else None
    na = H_A * HD
    oa = (o1_ref[...] + o2_ref[...] + o3_ref[...]).astype(BF16)
    h2 = (h_ref[...] + _dot(oa, wout_ref[0:na, :]) + _dot(ob_ref[...].astype(BF16), wout_ref[na:2 * na, :])
          + _dot(oc_ref[...].astype(BF16), wout_ref[2 * na:, :]))
    h2_ref[...] = h2
    f = _rms(h2, gffn_ref[...])
    f_ref[...] = f.astype(BF16)
    if n_exp:
        logits = jnp.dot(f, router_ref[...], precision=HIGHEST, preferred_element_type=F32)
        lane = lax.broadcasted_iota(I32, logits.shape, 1)
        logits = jnp.where(lane < n_exp, logits, -jnp.inf)
        v1 = jnp.max(logits, axis=-1, keepdims=True)
        i1 = jnp.min(jnp.where(logits == v1, lane, LANES), axis=-1, keepdims=True)
        rest = jnp.where(lane == i1, -jnp.inf, logits)
        v2 = jnp.max(rest, axis=-1, keepdims=True)
        i2 = jnp.min(jnp.where(rest == v2, lane, LANES), axis=-1, keepdims=True)
        e2 = jnp.exp(v2 - v1)
        w_ref[...] = jnp.where(lane == i1, 1.0 / (1.0 + e2), 0.0) + jnp.where(lane == i2, e2 / (1.0 + e2), 0.0)


def _mix_proj(h, o1, o2, o3, ob, oc, wout, gffn, router, tm):
    m, d = h.shape
    n_exp = 0 if router is None else router[1]
    row = lambda i: (i, 0)
    full = lambda i: (0, 0)
    operands = [h, o1, o2, o3, ob, oc, wout, gffn.reshape(1, d)]
    in_specs = [pl.BlockSpec((tm, d), row)] + [pl.BlockSpec((tm, x.shape[1]), row) for x in (o1, o2, o3, ob, oc)]
    in_specs += [pl.BlockSpec(wout.shape, full), pl.BlockSpec((1, d), full)]
    out_specs = [pl.BlockSpec((tm, d), row), pl.BlockSpec((tm, d), row)]
    out_shape = [jax.ShapeDtypeStruct((m, d), F32), jax.ShapeDtypeStruct((m, d), BF16)]
    if n_exp:
        operands.append(router[0])
        in_specs.append(pl.BlockSpec(router[0].shape, full))
        out_specs.append(pl.BlockSpec((tm, LANES), row))
        out_shape.append(jax.ShapeDtypeStruct((m, LANES), F32))
    return pl.pallas_call(functools.partial(_mix_proj_kernel, n_exp), grid=(m // tm,), in_specs=in_specs,
                          out_specs=out_specs, out_shape=out_shape, compiler_params=_cparams(1),
                          name="mix_proj")(*operands)


def _ffn_kernel(cfg, *refs):
    moe, final, n_exp = cfg["moe"], cfg["final"], cfg["n_exp"]
    it = iter(refs)
    h2_ref, f_ref = next(it), next(it)
    w_ref = next(it) if moe else None
    wg_ref, wu_ref, wd_ref, gple_ref, pgate_ref, pproj_ref, pemb_ref = (next(it) for _ in range(7))
    gfin_ref = next(it) if final else None
    out_ref = next(it)
    y_ref = next(it) if final else None
    acc_s = next(it)
    if moe:
        tot_s = next(it)
        e, j, nf = pl.program_id(1), pl.program_id(2), pl.num_programs(2)
    else:
        e, j, nf = 0, pl.program_id(1), pl.num_programs(1)

    if moe:
        @pl.when((j == 0) & (e == 0))
        def _zero_total():
            tot_s[...] = jnp.zeros(tot_s.shape, F32)

    @pl.when(j == 0)
    def _zero():
        acc_s[...] = jnp.zeros(acc_s.shape, F32)

    fb = f_ref[...]
    wg, wu, wd = (wg_ref[0], wu_ref[0], wd_ref[0]) if moe else (wg_ref[...], wu_ref[...], wd_ref[...])
    gg = _dot(fb, wg)
    act = (gg * _sigmoid(gg)) * _dot(fb, wu)
    acc_s[...] += _dot(act.astype(BF16), wd)

    if moe:
        @pl.when(j == nf - 1)
        def _mix():
            w = w_ref[...]
            lane = lax.broadcasted_iota(I32, w.shape, 1)
            w_e = jnp.sum(jnp.where(lane == e, w, 0.0), axis=-1, keepdims=True)
            tot_s[...] += w_e * acc_s[...]
    last = (j == nf - 1) if not moe else ((j == nf - 1) & (e == n_exp - 1))

    @pl.when(last)
    def _epilogue():
        hn = h2_ref[...] + (tot_s[...] if moe else acc_s[...])
        gate = _sigmoid(_dot(_rms(hn, gple_ref[...]).astype(BF16), pgate_ref[...]))
        h3 = hn + gate * _dot(pemb_ref[...].astype(BF16), pproj_ref[...])
        out_ref[...] = h3
        if final:
            y_ref[...] = _rms(h3, gfin_ref[...])


def _ffn(h2, f, w_route, ffn, gple, pgate, pproj, pemb, gfin, tm, tf):
    m, d = h2.shape
    moe = w_route is not None
    dff = ffn[0].shape[-1]
    final = gfin is not None
    n_exp = ffn[0].shape[0] if moe else 1
    cfg = dict(moe=moe, final=final, n_exp=n_exp)
    if moe:
        grid = (m // tm, n_exp, dff // tf)
        row = lambda i, e, j: (i, 0)
        full = lambda i, e, j: (0, 0)
        wcol = pl.BlockSpec((1, d, tf), lambda i, e, j: (e, 0, j))
        wrow = pl.BlockSpec((1, tf, d), lambda i, e, j: (e, j, 0))
    else:
        grid = (m // tm, dff // tf)
        row = lambda i, j: (i, 0)
        full = lambda i, j: (0, 0)
        wcol = pl.BlockSpec((d, tf), lambda i, j: (0, j))
        wrow = pl.BlockSpec((tf, d), lambda i, j: (j, 0))
    vec = pl.BlockSpec((1, d), full)
    operands = [h2, f]
    in_specs = [pl.BlockSpec((tm, d), row), pl.BlockSpec((tm, d), row)]
    if moe:
        operands.append(w_route)
        in_specs.append(pl.BlockSpec((tm, LANES), row))
    operands += [ffn[0], ffn[1], ffn[2], gple.reshape(1, d), pgate, pproj, pemb]
    in_specs += [wcol, wcol, wrow, vec, pl.BlockSpec(pgate.shape, full), pl.BlockSpec(pproj.shape, full),
                 pl.BlockSpec((tm, pemb.shape[1]), row)]
    out_specs = [pl.BlockSpec((tm, d), row)]
    out_shape = [jax.ShapeDtypeStruct((m, d), F32)]
    if final:
        operands.append(gfin.reshape(1, d))
        in_specs.append(vec)
        out_specs.append(pl.BlockSpec((tm, d), row))
        out_shape.append(jax.ShapeDtypeStruct((m, d), F32))
    scratch = [pltpu.VMEM((tm, d), F32)] + ([pltpu.VMEM((tm, d), F32)] if moe else [])
    res = pl.pallas_call(functools.partial(_ffn_kernel, cfg), grid=grid, in_specs=in_specs, out_specs=out_specs,
                         out_shape=out_shape, scratch_shapes=scratch, compiler_params=_cparams(len(grid)),
                         name="ffn_moe" if moe else "ffn_dense")(*operands)
    return (res[0], res[1]) if final else (res[0], None)


def _post(h, o1, o2, o3, ob, oc, wout, gffn, ffn, gple, pgate, pproj, pemb, gfin, moe, tm, tf):
    router = (ffn[3], ffn[0].shape[0]) if moe else None
    res = _mix_proj(h, o1, o2, o3, ob, oc, wout, gffn, router, tm)
    return _ffn(res[0], res[1], res[2] if moe else None, ffn, gple, pgate, pproj, pemb, gfin, tm, tf)


def _split_offsets():
    sizes = (H_A * HD, W_NSA, W_NSA, W_NSA, 3 * H_A, H_B * HD, H_B * HD, H_B * HD, H_C * DV_C, H_C * DV_C, H_C * DV_C)
    offs, s = [], 0
    for n in sizes:
        offs.append((s, s + n))
        s += n
    return offs


def _layer_weights(i, w_in, k_w1, k_w2, k_pe, v_w1, v_w2, v_pe):
    (qa, kvc, kvs, kvw, ga, qb, kb, vb, qc, kc, vc) = _split_offsets()
    w = w_in[i]
    sl = lambda r: w[:, r[0]:r[1]]
    wq = jnp.concatenate([sl(qa), sl(qb), sl(qc), jnp.pad(sl(ga), ((0, 0), (0, LANES - 3 * H_A)))], axis=1).astype(BF16)
    wkv = jnp.concatenate([sl(kvc), sl(kvs), sl(kvw), sl(kb), sl(vb), sl(kc), sl(vc)], axis=1).T.astype(BF16)
    kd = STRIDE * HD

    def cmp_w(w1, w2, pe):
        w1i = w1[i]
        w1s = jnp.concatenate([w1i[:STRIDE], w1i[STRIDE:]], axis=2)
        zero = jnp.zeros_like(w1s)
        wbd = jnp.concatenate([jnp.concatenate([w1s, zero], axis=2), jnp.concatenate([zero, w1s], axis=2)], axis=1)
        pe8 = jnp.broadcast_to(pe[i].reshape(1, 2 * kd), (SUBLANES, 2 * kd))
        return wbd.astype(BF16), w1i.reshape(2 * kd, CMP_HID).astype(BF16), pe8, w2[i].astype(BF16)

    ck, cv = cmp_w(k_w1, k_w2, k_pe), cmp_w(v_w1, v_w2, v_pe)
    cmp_weights = tuple(jnp.stack([a, b]) for a, b in zip(ck, cv))
    return wq, wkv, cmp_weights


def _from_feature_major(x_t, heads, hd):
    b, _, t = x_t.shape
    return x_t.reshape(b, 2, heads, hd, t).transpose(0, 4, 1, 2, 3)


def _pool_view(cache):
    d, n, pg, two, h, hd = cache.shape
    return cache.transpose(0, 1, 3, 4, 5, 2).reshape(d * n, two * h * hd, pg)


def _stack_rows(x, n_heads, width, col_of_head, k_width):
    b = x.shape[0]
    out = jnp.zeros((b, n_heads, SUBLANES, k_width), x.dtype)
    for h in range(n_heads):
        out = out.at[:, h, :, col_of_head(h):col_of_head(h) + width].set(x[:, :, h * width:(h + 1) * width])
    return out.reshape(b, n_heads * SUBLANES, k_width)


def kernel(x_prompt, x_sample, cache_nsa_cmp, cache_nsa_sel, cache_nsa_win, cache_moba, cache_diff, page_table, p_prompt, p_sample, g_attn, w_in, w_out, nsa_cmp_k_w1, nsa_cmp_k_w2, nsa_cmp_k_pe, nsa_cmp_v_w1, nsa_cmp_v_w2, nsa_cmp_v_pe, diff_lq1, diff_lk1, diff_lq2, diff_lk2, diff_norm_g, g_ffn, ffn_w_gate, ffn_w_up, ffn_w_down, moe_router, moe_w_gate, moe_w_up, moe_w_down, ple_gate, ple_proj, g_ple, g_final):
    depth = w_in.shape[0]
    bp, tp, d = x_prompt.shape
    bs, ts, _ = x_sample.shape
    n_pool, page = cache_nsa_cmp.shape[1], cache_nsa_cmp.shape[2]
    n_pages = page_table.shape[1]
    past_len = n_pages * page
    w_buf = cache_nsa_win.shape[2]
    assert ts <= SUBLANES and tp % 512 == 0 and n_pages % 16 == 0 and past_len % MOBA_BLOCK == 0
    pt_flat = page_table.reshape(-1).astype(I32)
    pools = {"cmp": _pool_view(cache_nsa_cmp), "sel": _pool_view(cache_nsa_sel),
             "moba": _pool_view(cache_moba), "diff": _pool_view(cache_diff)}
    win_cache_t = cache_nsa_win.transpose(0, 1, 3, 4, 5, 2).reshape(depth, bs, W_NSA, w_buf)
    per_step = 16

    tab_p = _rope_tables(jnp.arange(tp, dtype=I32))
    tab_s = _rope_tables(past_len + (jnp.arange(bs * ts, dtype=I32) % ts))

    hp = x_prompt.reshape(bp * tp, d)
    hs = x_sample.reshape(bs * ts, d)
    outs_p = {k: [] for k in ("cmp", "sel", "win", "moba", "diff")}
    outs_s = {k: [] for k in ("cmp", "sel", "win", "moba", "diff")}
    y_p = y_s = None
    n_dense = 0
    for i in range(depth):
        wq, wkv, cmpw = _layer_weights(i, w_in, nsa_cmp_k_w1, nsa_cmp_k_w2, nsa_cmp_k_pe,
                                       nsa_cmp_v_w1, nsa_cmp_v_w2, nsa_cmp_v_pe)
        lam_init = 0.8 - 0.6 * math.exp(-0.3 * i)
        dpar = jnp.zeros((SUBLANES, LANES), F32)
        for r, v in enumerate((diff_lq1, diff_lk1, diff_lq2, diff_lk2)):
            dpar = dpar.at[r, :DH_C].set(v[i].astype(F32))
        diff_par = (dpar, diff_norm_g[i].reshape(1, DV_C))
        moe = i % 2 == 1
        jf = i // 2
        if moe:
            router = jnp.pad(moe_router[jf].astype(F32), ((0, 0), (0, LANES - moe_router.shape[2])))
            ffn = (moe_w_gate[jf].astype(BF16), moe_w_up[jf].astype(BF16), moe_w_down[jf].astype(BF16), router)
        else:
            ffn = (ffn_w_gate[jf].astype(BF16), ffn_w_up[jf].astype(BF16), ffn_w_down[jf].astype(BF16))
        dff = ffn[0].shape[-1]
        tf = dff // 2 if (dff // 2) % LANES == 0 else dff
        wout = w_out[i].astype(BF16)
        pgate, pproj = ple_gate[i].astype(BF16), ple_proj[i].astype(BF16)
        gfin = g_final if i == depth - 1 else None

        zq, cmp_t, sel_t, win_t, moba_t, diff_t = _proj_in(hp, g_attn[i], wq, wkv, tab_p, bp, tp, 512)
        for k, v in zip(("cmp", "sel", "win", "moba", "diff"), (cmp_t, sel_t, win_t, moba_t, diff_t)):
            outs_p[k].append(v)
        zq3 = zq.reshape(bp, tp, NQ)
        n_cmp = (tp - L_CMP) // STRIDE + 1
        kvcmp = _compress_tail(_half_proj(cmp_t, cmpw[0]), *cmpw[1:])
        tq = 512
        o_cmp, bm = _cmp_attn(zq3, kvcmp, n_cmp, tp // L_SEL, 0, tq)
        qa_blk, qb_blk, qc_blk = (tq, H_A * HD, 0), (tq, H_B * HD, 1), (tq, H_C * DV_C, QC0 // (H_C * DV_C))
        o_sel = _flash("sel", zq3, [sel_t], t0=0, out_w=H_A * HD, stacked=False, q_block=qa_blk, gates=zq3, bm=bm, tile=512)
        o_win = _flash("win", zq3, [win_t], t0=0, out_w=H_A * HD, stacked=False, q_block=qa_blk, gates=zq3, tile=512)
        means = _means(moba_t)
        o_b = _flash("moba", zq3, [moba_t], t0=0, out_w=H_B * HD, stacked=False, q_block=qb_blk, means=means,
                     tile=512, n_blk=tp // MOBA_BLOCK)
        o_c = _flash("diff", zq3, [diff_t], t0=0, out_w=H_C * DV_C, stacked=False, q_block=qc_blk, diff_par=diff_par,
                     tile=512, lam_init=lam_init)
        flat = lambda x: x.reshape(bp * tp, x.shape[-1])
        hp, y = _post(hp, flat(o_cmp), flat(o_sel), flat(o_win), flat(o_b), flat(o_c), wout, g_ffn[i], ffn,
                      g_ple[i], pgate, pproj, p_prompt[i].reshape(bp * tp, -1), gfin, moe, 512, tf)
        y_p = y if y is not None else y_p

        ms = bs * ts
        zq, cmp_n, sel_n, win_n, moba_n, diff_n = _proj_in(hs, g_attn[i], wq, wkv, tab_s, 1, ms, ms)
        news = dict(cmp=cmp_n, sel=sel_n, win=win_n, moba=moba_n, diff=diff_n)
        for k, v in news.items():
            outs_s[k].append(v)

        def tail_of(x):
            w = x.shape[1]
            return jnp.pad(x.reshape(w, bs, ts).transpose(1, 0, 2), ((0, 0), (0, 0), (0, LANES - ts)))

        zq8 = jnp.pad(zq.reshape(bs, ts, NQ), ((0, 0), (0, SUBLANES - ts), (0, 0)))
        l_tot = past_len + ts
        n_cmp = (l_tot - L_CMP) // STRIDE + 1
        assert (n_cmp + 1) * STRIDE == past_len
        half = _half_proj(None, cmpw[0], paged=(pt_flat, pools["cmp"], i * n_pool, n_pages, per_step))
        kvcmp = _compress_tail(half, *cmpw[1:])
        n_sel = -(-l_tot // L_SEL)
        o_cmp, bm = _cmp_attn(zq8, kvcmp, n_cmp, n_sel, past_len, SUBLANES)
        q_sel = _stack_rows(zq8[:, :, QA0:QA0 + H_A * HD], H_A, HD, lambda h: (h // HG_A) * HD, G_A * HD)
        bm_rows = jnp.repeat(bm, HG_A, axis=1).reshape(bs, H_A * SUBLANES, bm.shape[3])
        blk_step = per_step * page // L_SEL
        n_steps = n_pages // per_step
        assert n_sel == n_steps * blk_step + 1 and blk_step <= LANES
        bm_steps = bm_rows[:, :, :n_steps * blk_step].reshape(bs, H_A * SUBLANES, n_steps, blk_step).transpose(0, 2, 1, 3)
        bm_st = jnp.concatenate([jnp.pad(bm_steps, ((0, 0), (0, 0), (0, 0), (0, LANES - blk_step))),
                                 jnp.pad(bm_rows[:, None, :, n_steps * blk_step:n_sel],
                                         ((0, 0), (0, 0), (0, 0), (0, LANES - 1)))], axis=1)
        ga8 = zq8
        o_sel = _flash("sel", q_sel, None, t0=past_len, out_w=H_A * HD, stacked=True,
                       q_block=(H_A * SUBLANES, G_A * HD, 0), gates=ga8, bm=bm_st,
                       paged=(pt_flat, pools["sel"], i * n_pool, n_pages, per_step), tail=tail_of(sel_n))
        qa8_blk = (SUBLANES, H_A * HD, 0)
        o_win = _flash("win", zq8, [win_cache_t[i], tail_of(win_n)], t0=past_len, out_w=H_A * HD, stacked=False,
                       q_block=qa8_blk, gates=ga8, kv_k0=[past_len - w_buf, past_len])
        means = _means(None, paged=(pt_flat, pools["moba"], i * n_pool, n_pages, per_step))
        q_moba = _stack_rows(zq8[:, :, QB0:QB0 + H_B * HD], H_B, HD, lambda h: h * HD, H_B * HD)
        o_b = _flash("moba", q_moba, None, t0=past_len, out_w=H_B * HD, stacked=True,
                     q_block=(H_B * SUBLANES, H_B * HD, 0), means=means,
                     paged=(pt_flat, pools["moba"], i * n_pool, n_pages, per_step), tail=tail_of(moba_n),
                     n_blk=-(-l_tot // MOBA_BLOCK))
        q_diff = _stack_rows(zq8[:, :, QC0:QC0 + H_C * DV_C], 2 * H_C, DH_C, lambda u: u * DH_C, H_C * DV_C)
        o_c = _flash("diff", q_diff, None, t0=past_len, out_w=H_C * DV_C, stacked=True,
                     q_block=(2 * H_C * SUBLANES, H_C * DV_C, 0), diff_par=diff_par,
                     paged=(pt_flat, pools["diff"], i * n_pool, n_pages, per_step), tail=tail_of(diff_n),
                     lam_init=lam_init)
        flat = lambda x: x[:, :ts].reshape(ms, x.shape[-1])
        hs, y = _post(hs, flat(o_cmp), flat(o_sel), flat(o_win), flat(o_b), flat(o_c), wout, g_ffn[i], ffn,
                      g_ple[i], pgate, pproj, p_sample[i].reshape(ms, -1), gfin, moe, ms, tf)
        y_s = y if y is not None else y_s

    heads = dict(cmp=(G_A, HD), sel=(G_A, HD), win=(G_A, HD), moba=(H_B, HD), diff=(H_C, DV_C))
    res_p, res_s = {}, {}
    for k, (nh, hd) in heads.items():
        res_p[k] = jnp.stack([_from_feature_major(x, nh, hd) for x in outs_p[k]], axis=0)
        res_s[k] = jnp.stack([x[0].T.reshape(bs, ts, 2, nh, hd) for x in outs_s[k]], axis=0)
    keep = min(WINDOW, tp)
    win_p = res_p["win"][:, :, tp - keep:]
    win_all = jnp.concatenate([cache_nsa_win, res_s["win"]], axis=2)
    win_s = win_all[:, :, win_all.shape[2] - min(WINDOW, win_all.shape[2]):]
    return (y_p.reshape(bp, tp, d), y_s.reshape(bs, ts, d), res_p["cmp"], res_p["sel"], win_p, res_p["moba"],
            res_p["diff"], res_s["cmp"], res_s["sel"], win_s, res_s["moba"], res_s["diff"])
```

```python
import functools
import math

import jax
import jax.numpy as jnp
from jax import lax
from jax.experimental import pallas as pl
from jax.experimental.pallas import tpu as pltpu

F32 = jnp.float32
BF16 = jnp.bfloat16
I32 = jnp.int32

HD = 64
H_A = 6
G_A = 2
HG_A = H_A // G_A
H_B = 6
H_C = 4
DH_C = 32
DV_C = 2 * DH_C
ROPE_THETA = 500000.0
L_CMP = 32
STRIDE = 16
L_SEL = 64
SEL_RATIO = L_SEL // STRIDE
N_SEL_TOP = 16
N_LOCAL = 2
WINDOW = 512
CMP_HID = 2 * HD
MOBA_BLOCK = 256
MOBA_TOPK = 3
TOP_K = 2
EPS = 1e-6
BIG = 1e9
NEG = -1e30
LOG2E = math.log2(math.e)

LANES = 128
SUBLANES = 8
VMEM_LIMIT_BYTES = 56 * 1024 * 1024

QA0, QB0, QC0, GA0 = 0, H_A * HD, (H_A + H_B) * HD, (H_A + H_B) * HD + H_C * DV_C
NQ = GA0 + LANES
W_NSA = 2 * G_A * HD
W_MOBA = 2 * H_B * HD
W_DIFF = 2 * H_C * DV_C
CMP0, SEL0, WIN0, MOBA0 = 0, W_NSA, 2 * W_NSA, 3 * W_NSA
DIFF0 = MOBA0 + W_MOBA
NKV = DIFF0 + W_DIFF

HIGHEST = lax.Precision.HIGHEST
NT = (((1,), (1,)), ((), ()))


def _cparams(n_grid):
    return pltpu.CompilerParams(dimension_semantics=("arbitrary",) * n_grid,
                                vmem_limit_bytes=VMEM_LIMIT_BYTES)


def _dot(a, b):
    return jnp.dot(a, b, preferred_element_type=F32)


def _dot_nt(a, b):
    return lax.dot_general(a, b, NT, preferred_element_type=F32)


def _sigmoid(x):
    return 1.0 / (1.0 + jnp.exp(-x))


def _rms(x, g):
    return (x * lax.rsqrt(jnp.mean(x * x, axis=-1, keepdims=True) + EPS)) * g


def _topk_mask(score, k, n_valid):
    lane = lax.broadcasted_iota(I32, score.shape, 1)
    rank = jnp.zeros(score.shape, I32)
    for i in range(n_valid):
        col = jnp.broadcast_to(score[:, i:i + 1], score.shape)
        beats = (col > score) | ((col == score) & (lane > i))
        rank = rank + beats.astype(I32)
    return (rank < k) & (lane < n_valid)


def _proj_in_kernel(x_ref, g_ref, wq_ref, wkv_ref, c64_ref, sa64_ref, sb64_ref, c32_ref, sa32_ref, sb32_ref,
                    ck_ref, sk_ref, ck32_ref, sk32_ref, zq_ref, cmp_ref, sel_ref, win_ref, moba_ref, diff_ref):
    ab = _rms(x_ref[...], g_ref[...]).astype(BF16)

    def rope_q(z, c, sa, sb, half):
        return z * c + pltpu.roll(z, half, 1) * sa + pltpu.roll(z, LANES - half, 1) * sb

    for c0, width, kind in ((QA0, H_A * HD, 64), (QB0, H_B * HD, 64), (QC0, H_C * DV_C, 32), (GA0, LANES, 0)):
        z = _dot(ab, wq_ref[:, c0:c0 + width])
        for k in range(width // LANES):
            zc = z[:, k * LANES:(k + 1) * LANES]
            if kind == 64:
                zc = rope_q(zc, c64_ref[...], sa64_ref[...], sb64_ref[...], 8)
            elif kind == 32:
                zc = rope_q(zc, c32_ref[...], sa32_ref[...], sb32_ref[...], 4)
            else:
                zc = _sigmoid(zc)
            zq_ref[:, c0 + k * LANES:c0 + (k + 1) * LANES] = zc

    ck, sk, ck32, sk32 = ck_ref[...], sk_ref[...], ck32_ref[...], sk32_ref[...]

    def rope_k64(zh):
        x1, x2 = zh[0:8], zh[8:16]
        return jnp.concatenate([x1 * ck - x2 * sk, x2 * ck + x1 * sk, zh[16:HD]], axis=0)

    def rope_k32(zh):
        x = zh[0:8]
        return jnp.concatenate([x * ck32 + pltpu.roll(x, 4, 0) * sk32, zh[8:DH_C]], axis=0)

    for r0, width, o_ref, unit, n_rot in ((CMP0, W_NSA, cmp_ref, HD, G_A), (SEL0, W_NSA, sel_ref, HD, G_A),
                                          (WIN0, W_NSA, win_ref, HD, G_A), (MOBA0, W_MOBA, moba_ref, HD, H_B),
                                          (DIFF0, W_DIFF, diff_ref, DH_C, 2 * H_C)):
        z = _dot_nt(wkv_ref[r0:r0 + width, :], ab)
        parts = []
        for u in range(n_rot):
            zh = z[u * unit:(u + 1) * unit]
            parts.append(rope_k64(zh) if unit == HD else rope_k32(zh))
        parts.append(z[n_rot * unit:width])
        o_ref[0] = jnp.concatenate(parts, axis=0)


def _rope_tables(pos):
    posf = pos.astype(F32)[:, None]

    def ang(rot):
        half = rot // 2
        inv = ROPE_THETA ** (-2.0 * jnp.arange(half, dtype=F32) / rot)
        a = posf * inv
        return jnp.cos(a), jnp.sin(a)

    def token_major(period, cos, sin):
        half = cos.shape[1]
        lane = jnp.arange(LANES) % period
        idx = jnp.where(lane < half, lane, lane - half) % half
        lo, hi = lane < half, (lane >= half) & (lane < 2 * half)
        c = jnp.where(lo | hi, cos[:, idx], 1.0)
        sa = jnp.where(hi, sin[:, idx], 0.0)
        sb = jnp.where(lo, -sin[:, idx], 0.0)
        return c, sa, sb

    cos64, sin64 = ang(HD // 4)
    cos32, sin32 = ang(DH_C // 4)
    t64 = token_major(HD, cos64, sin64)
    t32 = token_major(DH_C, cos32, sin32)
    ck, sk = cos64.T, sin64.T
    ck32 = jnp.concatenate([cos32, cos32], axis=1).T
    sk32 = jnp.concatenate([-sin32, sin32], axis=1).T
    return t64 + t32 + (ck, sk, ck32, sk32)


def _proj_in(h, g, wq, wkv, tables, n_seq, t_seq, tm):
    m, d = h.shape
    per = t_seq // tm
    row = lambda i: (i, 0)
    tab = lambda i: (i % per, 0)
    tabk = lambda i: (0, i % per)
    fm = lambda i: (i // per, 0, i % per)
    full = lambda i: (0, 0)
    in_specs = [pl.BlockSpec((tm, d), row), pl.BlockSpec((1, d), full),
                pl.BlockSpec((d, NQ), full), pl.BlockSpec((NKV, d), full)]
    in_specs += [pl.BlockSpec((tm, LANES), tab)] * 6 + [pl.BlockSpec((SUBLANES, tm), tabk)] * 4
    widths = (W_NSA, W_NSA, W_NSA, W_MOBA, W_DIFF)
    out_specs = [pl.BlockSpec((tm, NQ), row)] + [pl.BlockSpec((1, w, tm), fm) for w in widths]
    out_shape = [jax.ShapeDtypeStruct((m, NQ), F32)] + [jax.ShapeDtypeStruct((n_seq, w, t_seq), F32) for w in widths]
    return pl.pallas_call(_proj_in_kernel, grid=(m // tm,), in_specs=in_specs, out_specs=out_specs,
                          out_shape=out_shape, compiler_params=_cparams(1), name="proj_in")(
        h, g.reshape(1, d), wq, wkv, *tables)


def _half_proj_kernel(cfg, *refs):
    refs = list(refs)
    if cfg["paged"]:
        refs.pop(0)
    n_kv = cfg["n_kv"]
    kv_refs, w_ref, o_ref, rows_s = refs[:n_kv], refs[n_kv], refs[n_kv + 1], refs[n_kv + 2]
    chunks = [(r, c) for r in kv_refs for c in range(r.shape[2] // LANES)]
    n_half = len(chunks) * LANES // STRIDE
    for kv in range(2):
        for n, (r, c) in enumerate(chunks):
            rows_s[n * LANES:(n + 1) * LANES, :] = r[0, kv * G_A * HD:(kv + 1) * G_A * HD, c * LANES:(c + 1) * LANES].T
        acc = jnp.zeros((n_half, 2 * G_A * CMP_HID), F32)
        for s in range(STRIDE):
            acc = acc + _dot(rows_s[pl.ds(s, n_half, stride=STRIDE), :].astype(BF16), w_ref[kv, s])
        o_ref[0, kv] = acc


def _half_proj(kvs, wbd, paged=None):
    if paged is not None:
        page_table, pool, row0, n_pages, per_step = paged
        b = page_table.shape[0] // n_pages
        w, pg = pool.shape[1], pool.shape[2]
        in_specs = [pl.BlockSpec((1, w, pg), lambda bi, j, pt, n=n: (row0 + pt[bi * n_pages + j * per_step + n], 0, 0))
                    for n in range(per_step)]
        operands = [page_table] + [pool] * per_step
        n_kv, grid, npre, step_tokens, total = per_step, (b, n_pages // per_step), 1, per_step * pg, n_pages * pg
    else:
        x = kvs
        b, w, total = x.shape
        in_specs = [pl.BlockSpec((1, w, total), lambda bi, j: (bi, 0, 0))]
        operands = [x]
        n_kv, grid, npre, step_tokens = 1, (b, 1), 0, total
    in_specs.append(pl.BlockSpec(wbd.shape, lambda bi, j, *_: (0, 0, 0, 0)))
    wo = 2 * G_A * CMP_HID
    grid_spec = pltpu.PrefetchScalarGridSpec(
        num_scalar_prefetch=npre, grid=grid, in_specs=in_specs,
        out_specs=pl.BlockSpec((1, 2, step_tokens // STRIDE, wo), lambda bi, j, *_: (bi, 0, j, 0)),
        scratch_shapes=[pltpu.VMEM((step_tokens, G_A * HD), F32)])
    return pl.pallas_call(functools.partial(_half_proj_kernel, dict(paged=paged is not None, n_kv=n_kv)),
                          grid_spec=grid_spec, out_shape=jax.ShapeDtypeStruct((b, 2, total // STRIDE, wo), F32),
                          compiler_params=_cparams(2), name="nsa_half_proj")(*operands, wbd)


def _compress_tail_kernel(a_ref, w1f_ref, pe_ref, w2_ref, o_ref):
    nh = a_ref.shape[2]
    a = a_ref[0, 0]
    pe_term = _dot(pe_ref[0].astype(BF16), w1f_ref[0])[0:1]
    hid = a[:, :CMP_HID] + pltpu.roll(a[:, CMP_HID:], nh - 1, 0) + pe_term
    act = hid * _sigmoid(hid)
    o_ref[0, 0, 0] = _dot(act.astype(BF16), w2_ref[0])


def _compress_tail(a, w1flat, pe8, w2):
    b, _, nh, _ = a.shape
    kd = STRIDE * HD
    return pl.pallas_call(
        _compress_tail_kernel, grid=(b, 2, G_A),
        in_specs=[pl.BlockSpec((1, 1, nh, 2 * CMP_HID), lambda bi, kv, gi: (bi, kv, 0, gi)),
                  pl.BlockSpec((1, 2 * kd, CMP_HID), lambda bi, kv, gi: (kv, 0, 0)),
                  pl.BlockSpec((1, SUBLANES, 2 * kd), lambda bi, kv, gi: (kv, 0, 0)),
                  pl.BlockSpec((1, CMP_HID, HD), lambda bi, kv, gi: (kv, 0, 0))],
        out_specs=pl.BlockSpec((1, 1, 1, nh, HD), lambda bi, kv, gi: (bi, kv, gi, 0, 0)),
        out_shape=jax.ShapeDtypeStruct((b, 2, G_A, nh, HD), F32),
        compiler_params=_cparams(3), name="nsa_compress")(a, w1flat, pe8, w2)


def _cmp_attn_kernel(cfg, q_ref, ga_ref, kv_ref, o_ref, bm_ref):
    tq, n_cmp, n_sel, t0 = cfg["tq"], cfg["n_cmp"], cfg["n_sel"], cfg["t0"]
    ncp, nsp = kv_ref.shape[3], bm_ref.shape[3]
    i = pl.program_id(1)
    qpos = t0 + i * tq + lax.broadcasted_iota(I32, (tq, 1), 0)
    q = q_ref[0]
    gates = ga_ref[0]
    n_idx = lax.broadcasted_iota(I32, (1, ncp), 1)
    valid = ((n_idx * STRIDE + (L_CMP - 1)) <= qpos) & (n_idx < n_cmp)
    nn = lax.broadcasted_iota(I32, (ncp, nsp), 0)
    jj = lax.broadcasted_iota(I32, (ncp, nsp), 1)
    blk = lax.shift_right_logical(nn, 2)
    overlap = ((blk == jj) | (((nn & 3) == 3) & (blk + 1 == jj))) & (nn < n_cmp)
    mmap = overlap.astype(F32)
    lane = lax.broadcasted_iota(I32, (tq, nsp), 1)
    qblk = lax.shift_right_logical(qpos, 6)
    causal = lane <= qblk
    forced = (lane == 0) | (lane > qblk - N_LOCAL)
    scale = HD ** -0.5
    for g in range(G_A):
        kc = kv_ref[0, 0, g]
        vc = kv_ref[0, 1, g].astype(BF16)
        imp = jnp.zeros((tq, ncp), F32)
        for z in range(HG_A):
            hcol = (g * HG_A + z) * HD
            s = lax.dot_general(q[:, hcol:hcol + HD], kc, NT, precision=HIGHEST, preferred_element_type=F32) * scale
            s = jnp.where(valid, s, NEG)
            e = jnp.where(valid, jnp.exp(s - jnp.max(s, axis=-1, keepdims=True)), 0.0)
            p = e / jnp.maximum(jnp.sum(e, axis=-1, keepdims=True), 1e-30)
            imp = imp + p
            gcol = g * HG_A + z
            o_ref[0, :, hcol:hcol + HD] = gates[:, gcol:gcol + 1] * _dot(p.astype(BF16), vc)
        imp_blk = jnp.dot(imp, mmap, precision=HIGHEST, preferred_element_type=F32)
        k_top = min(N_SEL_TOP, n_sel)
        if tq % LANES == 0:
            nb8 = -(-n_sel // SUBLANES) * SUBLANES
            imp_t = imp_blk.T[0:nb8]
            blk_t = lax.broadcasted_iota(I32, (nb8, tq), 0)
            qblk_t = lax.shift_right_logical(t0 + i * tq + lax.broadcasted_iota(I32, (1, tq), 1), 6)
            causal_t = blk_t <= qblk_t
            forced_t = (blk_t == 0) | (blk_t > qblk_t - N_LOCAL)
            score_t = jnp.where(causal_t, jnp.where(forced_t, BIG, imp_t), -BIG)
            rank = jnp.zeros((nb8, tq), I32)
            for ib in range(n_sel):
                row = score_t[ib:ib + 1, :]
                rank = rank + ((row > score_t) | ((row == score_t) & (blk_t > ib))).astype(I32)
            picked_t = ((rank < k_top) & causal_t & (blk_t < n_sel)).astype(F32)
            bm_ref[0, g] = jnp.concatenate([picked_t, jnp.zeros((nsp - nb8, tq), F32)], axis=0).T
        else:
            score = jnp.where(causal, jnp.where(forced, BIG, imp_blk), -BIG)
            picked = _topk_mask(score, k_top, n_sel) & causal
            bm_ref[0, g] = picked.astype(F32)


def _cmp_attn(zq3, kvcmp, n_cmp, n_sel, t0, tq):
    b, t, _ = zq3.shape
    ncp = kvcmp.shape[3]
    nsp = -(-n_sel // LANES) * LANES
    cfg = dict(tq=tq, n_cmp=n_cmp, n_sel=n_sel, t0=t0)
    return pl.pallas_call(
        functools.partial(_cmp_attn_kernel, cfg), grid=(b, t // tq),
        in_specs=[pl.BlockSpec((1, tq, H_A * HD), lambda bi, i: (bi, i, QA0 // (H_A * HD))),
                  pl.BlockSpec((1, tq, LANES), lambda bi, i: (bi, i, GA0 // LANES)),
                  pl.BlockSpec((1, 2, G_A, ncp, HD), lambda bi, i: (bi, 0, 0, 0, 0))],
        out_specs=[pl.BlockSpec((1, tq, H_A * HD), lambda bi, i: (bi, i, 0)),
                   pl.BlockSpec((1, G_A, tq, nsp), lambda bi, i: (bi, 0, i, 0))],
        out_shape=[jax.ShapeDtypeStruct((b, t, H_A * HD), F32), jax.ShapeDtypeStruct((b, G_A, t, nsp), F32)],
        compiler_params=_cparams(2), name="nsa_cmp_attn")(zq3, zq3, kvcmp)


def _means_kernel(cfg, *refs):
    paged, n_kv = cfg["paged"], cfg["n_kv"]
    refs = list(refs)
    if paged:
        refs.pop(0)
    kv_refs, o_ref, acc_s = refs[:n_kv], refs[n_kv], refs[n_kv + 1]
    j = pl.program_id(1)

    @pl.when(j == 0)
    def _():
        acc_s[...] = jnp.zeros(acc_s.shape, F32)

    chunks_per_block = MOBA_BLOCK // LANES
    lane = lax.broadcasted_iota(I32, acc_s.shape, 1)
    acc = acc_s[...]
    chunk0 = j * cfg["chunks_per_step"]
    for n, r in enumerate(kv_refs):
        x = r[0]
        for c in range(x.shape[1] // LANES):
            blk = (chunk0 + n * (x.shape[1] // LANES) + c) // chunks_per_block
            s = jnp.sum(x[:, c * LANES:(c + 1) * LANES], axis=1, keepdims=True)
            acc = acc + jnp.where(lane == blk, s, 0.0)
    acc_s[...] = acc

    @pl.when(j == pl.num_programs(1) - 1)
    def _():
        o_ref[0] = (acc_s[...] / float(MOBA_BLOCK)).T


def _flash_kernel(cfg, *refs):
    kind, jobs, rows = cfg["kind"], cfg["jobs"], cfg["rows"]
    stacked, t0 = cfg["stacked"], cfg["t0"]
    it = iter(refs)
    if cfg["paged"]:
        next(it)
    q_ref = next(it)
    ga_ref = next(it) if kind in ("sel", "win") else None
    bm_ref = next(it) if kind == "sel" else None
    mt_ref = next(it) if kind == "moba" else None
    dp_ref = next(it) if kind == "diff" else None
    ng_ref = next(it) if kind == "diff" else None
    kv_refs = [next(it) for _ in range(cfg["n_kv"])]
    tail_ref = next(it) if cfg["tail"] else None
    tail_bm_ref = next(it) if (kind == "sel" and cfg["tail"]) else None
    o_ref = next(it)
    m_s, l_s, acc_s = next(it), next(it), next(it)
    bm_s = next(it) if kind == "moba" else None
    kvb_s = next(it) if cfg["chunked"] else None

    i, j = pl.program_id(1), pl.program_id(2)
    nk = pl.num_programs(2)
    qscale = cfg["scale"] * LOG2E
    n_blk = cfg["n_blk"]

    def positions(r0, n, axis):
        rid = r0 + lax.broadcasted_iota(I32, (n, 1) if axis == 0 else (1, n), axis)
        return t0 + (rid & (SUBLANES - 1)) if stacked else t0 + i * rows + rid

    @pl.when(j == 0)
    def _init():
        m_s[...] = jnp.full(m_s.shape, NEG, F32)
        l_s[...] = jnp.zeros(l_s.shape, F32)
        acc_s[...] = jnp.zeros(acc_s.shape, F32)
        if kind == "moba":
            q = q_ref[0]
            k_top = min(MOBA_TOPK, n_blk)
            if stacked:
                lane = lax.broadcasted_iota(I32, (rows, LANES), 1)
                own = lax.shift_right_logical(positions(0, rows, 0), 8)
                past = lane < own
                gate = lax.dot_general(q, mt_ref[0], NT, precision=HIGHEST, preferred_element_type=F32)
                picked = _topk_mask(jnp.where(past, gate, -BIG), k_top, n_blk) & past
                bm_s[0] = (picked | (lane == own)).astype(F32)
            else:
                nb8 = -(-n_blk // SUBLANES) * SUBLANES
                blk = lax.broadcasted_iota(I32, (nb8, rows), 0)
                own = lax.shift_right_logical(positions(0, rows, 1), 8)
                past = blk < own
                for mi, (qc0, qw, kr0, kw) in enumerate(cfg["mask_jobs"]):
                    gate = lax.dot_general(mt_ref[0, 0:nb8, kr0:kr0 + kw], q[:, qc0:qc0 + qw], NT,
                                           precision=HIGHEST, preferred_element_type=F32)
                    score = jnp.where(past, gate, -BIG)
                    rank = jnp.zeros((nb8, rows), I32)
                    for ib in range(n_blk):
                        row = score[ib:ib + 1, :]
                        rank = rank + ((row > score) | ((row == score) & (blk > ib))).astype(I32)
                    picked = ((rank < k_top) & past) | (blk == own)
                    full = jnp.concatenate([picked.astype(F32), jnp.zeros((LANES - nb8, rows), F32)], axis=0)
                    bm_s[mi] = full.T

    def update(ji, rsl, qj, tiles, vw):
        ss = []
        for kt, _, bias in tiles:
            s = _dot(qj, kt)
            ss.append(s if bias is None else s + bias)
        m_cur = lane_fold(ss, jnp.maximum)
        m_cur = jnp.broadcast_to(jnp.max(m_cur, axis=-1, keepdims=True), m_cur.shape)
        m_prev = m_s[ji, rsl, :]
        m_new = jnp.maximum(m_prev, m_cur)
        alpha = jnp.exp2(m_prev - m_new)
        ps = [jnp.exp2(s - lanes_to(m_new, s.shape[1])) for s in ss]
        tot = lane_fold(ps, jnp.add)
        tot = jnp.broadcast_to(jnp.sum(tot, axis=-1, keepdims=True), tot.shape)
        l_s[ji, rsl, :] = alpha * l_s[ji, rsl, :] + tot
        m_s[ji, rsl, :] = m_new
        pv = _dot_nt(ps[0].astype(BF16), tiles[0][1])
        for p, (_, vt, _) in zip(ps[1:], tiles[1:]):
            pv = pv + _dot_nt(p.astype(BF16), vt)
        acc_s[ji, rsl, 0:vw] = lanes_to(alpha, vw) * acc_s[ji, rsl, 0:vw] + pv

    def lane_fold(xs, op):
        out = None
        for x in xs:
            for c in range(x.shape[1] // LANES):
                part = x[:, c * LANES:(c + 1) * LANES]
                out = part if out is None else op(out, part)
        return out

    def lanes_to(x, width):
        if width <= LANES:
            return x[:, 0:width]
        reps = -(-width // LANES)
        return pltpu.repeat(x, reps, axis=1)[:, 0:width]

    def causal_bias(qp, k0, tks):
        kpos = k0 + lax.broadcasted_iota(I32, (1, tks), 1)
        ok = kpos <= qp
        if kind == "win":
            ok = ok & (kpos >= qp - WINDOW)
        return jnp.where(ok, 0.0, NEG)

    def block_bias(bm_rows, expand):
        return (_dot(bm_rows.astype(BF16), expand) - 1.0) * (-NEG)

    def chunked_tile(k0):
        tk = kvb_s.shape[1]
        kvb_s[...] = kv_refs[0][0].astype(BF16)
        expand = None
        if kind in ("sel", "moba"):
            shift = 6 if kind == "sel" else 8
            kb = lax.shift_right_logical(k0 + lax.broadcasted_iota(I32, (LANES, tk), 1), shift)
            expand = (lax.broadcasted_iota(I32, (LANES, tk), 0) == kb).astype(BF16)

        n_chunk = rows // cfg["row_chunk"]

        def chunk(c, carry):
            if n_chunk == 1:
                r0, rsl = 0, slice(None)
            else:
                r0 = pl.multiple_of(c * cfg["row_chunk"], cfg["row_chunk"])
                rsl = pl.ds(r0, cfg["row_chunk"])
            qc = (q_ref[0, rsl, :] * qscale).astype(BF16)
            base = causal_bias(positions(r0, cfg["row_chunk"], 0), k0, tk)
            biases = {}
            for ji, (qc0, qw, kr0, kw, vr0, vw, mi) in enumerate(jobs):
                if mi not in biases:
                    if kind == "sel":
                        biases[mi] = base + block_bias(bm_ref[0, mi, rsl, :], expand)
                    elif kind == "moba":
                        biases[mi] = base + block_bias(bm_s[mi, rsl, :], expand)
                    else:
                        biases[mi] = base
                update(ji, rsl, qc[:, qc0:qc0 + qw],
                       [(kvb_s[kr0:kr0 + kw, :], kvb_s[vr0:vr0 + vw, :], biases[mi])], vw)
            return carry

        if n_chunk == 1:
            chunk(0, 0)
        else:
            lax.fori_loop(0, n_chunk, chunk, 0)

    def column_bias(col):
        return (col - 1.0) * (-NEG)

    def whole_tiles(pairs, page_cols=None, tail_col=None):
        qb = (q_ref[0] * qscale).astype(BF16)
        all_rows = slice(None)
        qp = positions(0, rows, 0)
        kvbs = [kv.astype(BF16) for kv, _ in pairs]
        for ji, (qc0, qw, kr0, kw, vr0, vw, mi) in enumerate(jobs):
            tiles = []
            for n, ((kv, k0), kvb) in enumerate(zip(pairs, kvbs)):
                bias = None if k0 is None else causal_bias(qp, k0, kv.shape[1])
                extra = page_cols[n] if page_cols is not None else tail_col
                if extra is not None:
                    bias = extra if bias is None else bias + extra
                tiles.append((kvb[kr0:kr0 + kw], kvb[vr0:vr0 + vw], bias))
            update(ji, all_rows, qb[:, qc0:qc0 + qw], tiles, vw)

    def normalized(ji, vw):
        return acc_s[ji, :, 0:vw] / jnp.maximum(lanes_to(l_s[ji], vw), 1e-30)

    def finalize():
        if kind in ("sel", "win"):
            gates = ga_ref[0]
            branch = 1 if kind == "sel" else 2
            for h in range(H_A):
                gcol = branch * H_A + h
                if stacked:
                    o = normalized(0, cfg["jobs"][0][5])[h * SUBLANES:(h + 1) * SUBLANES,
                                                         (h // HG_A) * HD:(h // HG_A + 1) * HD]
                else:
                    o = normalized(h, HD)
                o_ref[0, :, h * HD:(h + 1) * HD] = gates[:, gcol:gcol + 1] * o
        elif kind == "moba":
            for h in range(H_B):
                if stacked:
                    o = normalized(0, H_B * HD)[h * SUBLANES:(h + 1) * SUBLANES, h * HD:(h + 1) * HD]
                else:
                    o = normalized(h, HD)
                o_ref[0, :, h * HD:(h + 1) * HD] = o
        else:
            dp = dp_ref[...]
            lam_init = cfg["lam_init"]
            lam = (jnp.exp(jnp.sum(dp[0:1] * dp[1:2], axis=-1, keepdims=True))
                   - jnp.exp(jnp.sum(dp[2:3] * dp[3:4], axis=-1, keepdims=True)) + lam_init)
            ng = ng_ref[...]
            full = normalized(0, H_C * DV_C) if stacked else None
            for h in range(H_C):
                if stacked:
                    o1 = full[(2 * h) * SUBLANES:(2 * h + 1) * SUBLANES, h * DV_C:(h + 1) * DV_C]
                    o2 = full[(2 * h + 1) * SUBLANES:(2 * h + 2) * SUBLANES, h * DV_C:(h + 1) * DV_C]
                else:
                    o1, o2 = normalized(2 * h, DV_C), normalized(2 * h + 1, DV_C)
                o_ref[0, :, h * DV_C:(h + 1) * DV_C] = _rms(o1 - lam * o2, ng) * (1.0 - lam_init)

    if cfg["paged"]:
        n_kv = cfg["n_kv"]
        page_cols = None
        if kind == "sel":
            bm = bm_ref[0, 0]
            half = lax.broadcasted_iota(I32, (rows, LANES), 1) < L_SEL
            page_cols = [column_bias(jnp.where(half, bm[:, 2 * n:2 * n + 1], bm[:, 2 * n + 1:2 * n + 2]))
                         for n in range(n_kv)]
        elif kind == "moba":
            per_blk = MOBA_BLOCK // kv_refs[0].shape[2]
            bm = pltpu.roll(bm_s[0], (LANES - (n_kv // per_blk) * j) % LANES, 1)
            page_cols = [column_bias(bm[:, n // per_blk:n // per_blk + 1]) for n in range(n_kv)]
        whole_tiles([(r[0], None) for r in kv_refs], page_cols=page_cols)

        @pl.when(j == nk - 1)
        def _last():
            if tail_ref is not None:
                tail_col = None
                if kind == "sel":
                    tail_col = column_bias(tail_bm_ref[0, 0][:, 0:1])
                elif kind == "moba":
                    tail_col = column_bias(bm_s[0][:, n_blk - 1:n_blk])
                whole_tiles([(tail_ref[0], cfg["tail_k0"])], tail_col=tail_col)
            finalize()
    else:
        k0s = cfg["kv_k0"]
        if k0s is None:
            tk = kv_refs[0].shape[2]
            active = j * tk <= (t0 - cfg["k_base"]) + i * rows + rows - 1
            if kind == "win":
                active = active & ((j + 1) * tk - 1 >= (t0 - cfg["k_base"]) + i * rows - WINDOW)

            @pl.when(active)
            def _tile():
                chunked_tile(cfg["k_base"] + j * tk)
        else:
            whole_tiles([(r[0], k0) for r, k0 in zip(kv_refs, k0s)])

        @pl.when(j == nk - 1)
        def _last():
            finalize()


def _head_jobs(kind):
    if kind in ("sel", "win"):
        return [((g * HG_A + z) * HD, HD, g * HD, HD, G_A * HD + g * HD, HD, g) for g in range(G_A) for z in range(HG_A)]
    if kind == "moba":
        return [(h * HD, HD, h * HD, HD, H_B * HD + h * HD, HD, h) for h in range(H_B)]
    return [(h * DV_C + s * DH_C, DH_C, h * DV_C + s * DH_C, DH_C, H_C * DV_C + h * DV_C, DV_C, 0)
            for h in range(H_C) for s in range(2)]


def _stacked_jobs(kind):
    kc = {"sel": G_A * HD, "moba": H_B * HD, "diff": H_C * DV_C}[kind]
    return [(0, kc, 0, kc, kc, kc, 0)]


def _flash(kind, q, kvs, *, t0, out_w, stacked, q_block, gates=None, bm=None, means=None, diff_par=None,
           tile=None, kv_k0=None, k_base=0, paged=None, tail=None, n_blk=None, lam_init=None, row_chunk=None):
    b = q.shape[0]
    rows, qw, qblk = q_block
    nq = q.shape[1] // rows
    jobs = _stacked_jobs(kind) if stacked else _head_jobs(kind)
    vmax = max(jb[5] for jb in jobs)
    cfg = dict(kind=kind, jobs=jobs, rows=rows, stacked=stacked, t0=t0, paged=paged is not None,
               tail=tail is not None, kv_k0=kv_k0, k_base=k_base, n_blk=n_blk, lam_init=lam_init,
               chunked=tile is not None, row_chunk=row_chunk or rows,
               scale=(DH_C if kind == "diff" else HD) ** -0.5, n_mask=1 if stacked else (G_A if kind == "sel" else H_B))
    if kind == "moba":
        cfg["mask_jobs"] = [(jb[0], jb[1], jb[2], jb[3]) for jb in jobs]
    operands, in_specs = [], []
    npre = 0
    if paged is not None:
        page_table, pool, row0, n_pages, per_step = paged
        npre = 1
        nk = n_pages // per_step
        cfg["n_kv"] = per_step
        cfg["tail_k0"] = n_pages * pool.shape[2]
    elif tile is not None:
        nk = kvs[0].shape[2] // tile
        cfg["n_kv"] = 1
    else:
        nk = 1
        cfg["n_kv"] = len(kvs)

    def add(x, block, imap):
        operands.append(x)
        in_specs.append(pl.BlockSpec(block, imap))

    out_rows = SUBLANES if stacked else rows
    add(q, (1, rows, qw), lambda bi, i, j, *_: (bi, i, qblk))
    if kind in ("sel", "win"):
        add(gates, (1, out_rows, LANES), lambda bi, i, j, *_: (bi, i, gates.shape[2] // LANES - 1))
    if kind == "sel" and paged is not None:
        add(bm, (1, 1, rows, LANES), lambda bi, i, j, *_: (bi, j, 0, 0))
    elif kind == "sel":
        add(bm, (1, bm.shape[1], rows, bm.shape[3]), lambda bi, i, j, *_: (bi, 0, i, 0))
    if kind == "moba":
        add(means, (1,) + means.shape[1:], lambda bi, i, j, *_: (bi, 0, 0))
    if kind == "diff":
        add(diff_par[0], diff_par[0].shape, lambda bi, i, j, *_: (0, 0))
        add(diff_par[1], diff_par[1].shape, lambda bi, i, j, *_: (0, 0))
    if paged is not None:
        w, pg = pool.shape[1], pool.shape[2]
        for n in range(per_step):
            add(pool, (1, w, pg),
                lambda bi, i, j, pt, n=n: (row0 + pt[bi * n_pages + j * per_step + n], 0, 0))
    elif tile is not None:
        w = kvs[0].shape[1]

        def kv_map(bi, i, j, *_):
            hi = (t0 - k_base + i * rows + rows - 1) // tile
            lo = jnp.maximum(t0 - k_base + i * rows - WINDOW, 0) // tile if kind == "win" else 0
            return (bi, 0, jnp.clip(j, lo, hi))

        add(kvs[0], (1, w, tile), kv_map)
    else:
        for x in kvs:
            add(x, (1,) + x.shape[1:], lambda bi, i, j, *_: (bi, 0, 0))
    if tail is not None:
        add(tail, (1,) + tail.shape[1:], lambda bi, i, j, *_: (bi, 0, 0))
        if kind == "sel":
            add(bm, (1, 1, rows, LANES), lambda bi, i, j, *_: (bi, nk, 0, 0))
    scratch = [pltpu.VMEM((len(jobs), rows, LANES), F32), pltpu.VMEM((len(jobs), rows, LANES), F32),
               pltpu.VMEM((len(jobs), rows, vmax), F32)]
    if kind == "moba":
        scratch.append(pltpu.VMEM((cfg["n_mask"], rows, LANES), F32))
    if tile is not None:
        scratch.append(pltpu.VMEM((kvs[0].shape[1], tile), BF16))
    grid_spec = pltpu.PrefetchScalarGridSpec(
        num_scalar_prefetch=npre, grid=(b, nq, nk), in_specs=in_specs,
        out_specs=pl.BlockSpec((1, out_rows, out_w), lambda bi, i, j, *_: (bi, i, 0)), scratch_shapes=scratch)
    out_t = SUBLANES if stacked else q.shape[1]
    call = pl.pallas_call(functools.partial(_flash_kernel, cfg), grid_spec=grid_spec,
                          out_shape=jax.ShapeDtypeStruct((b, out_t, out_w), F32),
                          compiler_params=_cparams(3), name="attn_" + kind + ("_dec" if stacked or kv_k0 else ""))
    return call(*(([paged[0]] if paged is not None else []) + operands))


def _means(kvs, paged=None):
    if paged is not None:
        page_table, pool, row0, n_pages, per_step = paged
        b = page_table.shape[0] // n_pages
        kw, pg = pool.shape[1] // 2, pool.shape[2]
        cfg = dict(paged=True, n_kv=per_step, chunks_per_step=per_step * pg // LANES)
        in_specs = [pl.BlockSpec((1, kw, pg), lambda bi, j, pt, n=n: (row0 + pt[bi * n_pages + j * per_step + n], 0, 0))
                    for n in range(per_step)]
        operands = [page_table] + [pool] * per_step
        grid, npre = (b, n_pages // per_step), 1
    else:
        x = kvs
        b, kw = x.shape[0], x.shape[1] // 2
        cfg = dict(paged=False, n_kv=1, chunks_per_step=x.shape[2] // LANES)
        in_specs = [pl.BlockSpec((1, kw, x.shape[2]), lambda bi, j: (bi, 0, 0))]
        operands = [x]
        grid, npre = (b, 1), 0
    grid_spec = pltpu.PrefetchScalarGridSpec(
        num_scalar_prefetch=npre, grid=grid, in_specs=in_specs,
        out_specs=pl.BlockSpec((1, LANES, kw), lambda bi, j, *_: (bi, 0, 0)),
        scratch_shapes=[pltpu.VMEM((kw, LANES), F32)])
    return pl.pallas_call(functools.partial(_means_kernel, cfg), grid_spec=grid_spec,
                          out_shape=jax.ShapeDtypeStruct((b, LANES, kw), F32),
                          compiler_params=_cparams(2), name="moba_means")(*operands)


def _mix_proj_kernel(n_exp, *refs):
    it = iter(refs)
    h_ref, o1_ref, o2_ref, o3_ref, ob_ref, oc_ref, wout_ref, gffn_ref = (next(it) for _ in range(8))
    router_ref = next(it) if n_exp else None
    h2_ref, f_ref = next(it), next(it)
    w_ref = next(it) if n_exp else None
    na = H_A * HD
    oa = (o1_ref[...] + o2_ref[...] + o3_ref[...]).astype(BF16)
    h2 = (h_ref[...] + _dot(oa, wout_ref[0:na, :]) + _dot(ob_ref[...].astype(BF16), wout_ref[na:2 * na, :])
          + _dot(oc_ref[...].astype(BF16), wout_ref[2 * na:, :]))
    h2_ref[...] = h2
    f = _rms(h2, gffn_ref[...])
    f_ref[...] = f.astype(BF16)
    if n_exp:
        logits = jnp.dot(f, router_ref[...], precision=HIGHEST, preferred_element_type=F32)
        lane = lax.broadcasted_iota(I32, logits.shape, 1)
        logits = jnp.where(lane < n_exp, logits, -jnp.inf)
        v1 = jnp.max(logits, axis=-1, keepdims=True)
        i1 = jnp.min(jnp.where(logits == v1, lane, LANES), axis=-1, keepdims=True)
        rest = jnp.where(lane == i1, -jnp.inf, logits)
        v2 = jnp.max(rest, axis=-1, keepdims=True)
        i2 = jnp.min(jnp.where(rest == v2, lane, LANES), axis=-1, keepdims=True)
        e2 = jnp.exp(v2 - v1)
        w_ref[...] = jnp.where(lane == i1, 1.0 / (1.0 + e2), 0.0) + jnp.where(lane == i2, e2 / (1.0 + e2), 0.0)


def _mix_proj(h, o1, o2, o3, ob, oc, wout, gffn, router, tm):
    m, d = h.shape
    n_exp = 0 if router is None else router[1]
    row = lambda i: (i, 0)
    full = lambda i: (0, 0)
    operands = [h, o1, o2, o3, ob, oc, wout, gffn.reshape(1, d)]
    in_specs = [pl.BlockSpec((tm, d), row)] + [pl.BlockSpec((tm, x.shape[1]), row) for x in (o1, o2, o3, ob, oc)]
    in_specs += [pl.BlockSpec(wout.shape, full), pl.BlockSpec((1, d), full)]
    out_specs = [pl.BlockSpec((tm, d), row), pl.BlockSpec((tm, d), row)]
    out_shape = [jax.ShapeDtypeStruct((m, d), F32), jax.ShapeDtypeStruct((m, d), BF16)]
    if n_exp:
        operands.append(router[0])
        in_specs.append(pl.BlockSpec(router[0].shape, full))
        out_specs.append(pl.BlockSpec((tm, LANES), row))
        out_shape.append(jax.ShapeDtypeStruct((m, LANES), F32))
    return pl.pallas_call(functools.partial(_mix_proj_kernel, n_exp), grid=(m // tm,), in_specs=in_specs,
                          out_specs=out_specs, out_shape=out_shape, compiler_params=_cparams(1),
                          name="mix_proj")(*operands)


def _ffn_kernel(cfg, *refs):
    moe, final, n_exp = cfg["moe"], cfg["final"], cfg["n_exp"]
    it = iter(refs)
    h2_ref, f_ref = next(it), next(it)
    w_ref = next(it) if moe else None
    wgu_ref, wd_ref, gple_ref, pgate_ref, pproj_ref, pemb_ref = (next(it) for _ in range(6))
    gfin_ref = next(it) if final else None
    out_ref = next(it)
    y_ref = next(it) if final else None
    acc_s = next(it)
    if moe:
        tot_s = next(it)
        e, j, nf = pl.program_id(1), pl.program_id(2), pl.num_programs(2)
    else:
        e, j, nf = 0, pl.program_id(1), pl.num_programs(1)

    if moe:
        @pl.when((j == 0) & (e == 0))
        def _zero_total():
            tot_s[...] = jnp.zeros(tot_s.shape, F32)

    @pl.when(j == 0)
    def _zero():
        acc_s[...] = jnp.zeros(acc_s.shape, F32)

    fb = f_ref[...]
    wgu, wd = (wgu_ref[0], wd_ref[0]) if moe else (wgu_ref[...], wd_ref[...])
    z = _dot(fb, wgu)
    tf = z.shape[1] // 2
    gg = z[:, :tf]
    act = (gg * _sigmoid(gg)) * z[:, tf:]
    acc_s[...] += _dot(act.astype(BF16), wd)

    if moe:
        @pl.when(j == nf - 1)
        def _mix():
            w = w_ref[...]
            lane = lax.broadcasted_iota(I32, w.shape, 1)
            w_e = jnp.sum(jnp.where(lane == e, w, 0.0), axis=-1, keepdims=True)
            tot_s[...] += w_e * acc_s[...]
    last = (j == nf - 1) if not moe else ((j == nf - 1) & (e == n_exp - 1))

    @pl.when(last)
    def _epilogue():
        hn = h2_ref[...] + (tot_s[...] if moe else acc_s[...])
        gate = _sigmoid(_dot(_rms(hn, gple_ref[...]).astype(BF16), pgate_ref[...]))
        h3 = hn + gate * _dot(pemb_ref[...].astype(BF16), pproj_ref[...])
        out_ref[...] = h3
        if final:
            y_ref[...] = _rms(h3, gfin_ref[...])


def _ffn(h2, f, w_route, ffn, gple, pgate, pproj, pemb, gfin, tm, tf):
    m, d = h2.shape
    moe = w_route is not None
    dff = ffn[1].shape[-2]
    final = gfin is not None
    n_exp = ffn[0].shape[0] if moe else 1
    cfg = dict(moe=moe, final=final, n_exp=n_exp)
    if moe:
        grid = (m // tm, n_exp, dff // tf)
        row = lambda i, e, j: (i, 0)
        full = lambda i, e, j: (0, 0)
        wcol = pl.BlockSpec((1, d, 2 * tf), lambda i, e, j: (e, 0, j))
        wrow = pl.BlockSpec((1, tf, d), lambda i, e, j: (e, j, 0))
    else:
        grid = (m // tm, dff // tf)
        row = lambda i, j: (i, 0)
        full = lambda i, j: (0, 0)
        wcol = pl.BlockSpec((d, 2 * tf), lambda i, j: (0, j))
        wrow = pl.BlockSpec((tf, d), lambda i, j: (j, 0))
    vec = pl.BlockSpec((1, d), full)
    operands = [h2, f]
    in_specs = [pl.BlockSpec((tm, d), row), pl.BlockSpec((tm, d), row)]
    if moe:
        operands.append(w_route)
        in_specs.append(pl.BlockSpec((tm, LANES), row))
    operands += [ffn[0], ffn[1], gple.reshape(1, d), pgate, pproj, pemb]
    in_specs += [wcol, wrow, vec, pl.BlockSpec(pgate.shape, full), pl.BlockSpec(pproj.shape, full),
                 pl.BlockSpec((tm, pemb.shape[1]), row)]
    out_specs = [pl.BlockSpec((tm, d), row)]
    out_shape = [jax.ShapeDtypeStruct((m, d), F32)]
    if final:
        operands.append(gfin.reshape(1, d))
        in_specs.append(vec)
        out_specs.append(pl.BlockSpec((tm, d), row))
        out_shape.append(jax.ShapeDtypeStruct((m, d), F32))
    scratch = [pltpu.VMEM((tm, d), F32)] + ([pltpu.VMEM((tm, d), F32)] if moe else [])
    res = pl.pallas_call(functools.partial(_ffn_kernel, cfg), grid=grid, in_specs=in_specs, out_specs=out_specs,
                         out_shape=out_shape, scratch_shapes=scratch, compiler_params=_cparams(len(grid)),
                         name="ffn_moe" if moe else "ffn_dense")(*operands)
    return (res[0], res[1]) if final else (res[0], None)


def _post(h, o1, o2, o3, ob, oc, wout, gffn, ffn, gple, pgate, pproj, pemb, gfin, moe, tm, tf):
    router = (ffn[2], ffn[0].shape[0]) if moe else None
    res = _mix_proj(h, o1, o2, o3, ob, oc, wout, gffn, router, tm)
    return _ffn(res[0], res[1], res[2] if moe else None, ffn, gple, pgate, pproj, pemb, gfin, tm, tf)


def _split_offsets():
    sizes = (H_A * HD, W_NSA, W_NSA, W_NSA, 3 * H_A, H_B * HD, H_B * HD, H_B * HD, H_C * DV_C, H_C * DV_C, H_C * DV_C)
    offs, s = [], 0
    for n in sizes:
        offs.append((s, s + n))
        s += n
    return offs


def _layer_weights(i, w_in, k_w1, k_w2, k_pe, v_w1, v_w2, v_pe):
    (qa, kvc, kvs, kvw, ga, qb, kb, vb, qc, kc, vc) = _split_offsets()
    w = w_in[i]
    sl = lambda r: w[:, r[0]:r[1]]
    wq = jnp.concatenate([sl(qa), sl(qb), sl(qc), jnp.pad(sl(ga), ((0, 0), (0, LANES - 3 * H_A)))], axis=1).astype(BF16)
    wkv = jnp.concatenate([sl(kvc), sl(kvs), sl(kvw), sl(kb), sl(vb), sl(kc), sl(vc)], axis=1).T.astype(BF16)
    kd = STRIDE * HD

    def cmp_w(w1, w2, pe):
        w1i = w1[i]
        w1s = jnp.concatenate([w1i[:STRIDE], w1i[STRIDE:]], axis=2)
        zero = jnp.zeros_like(w1s)
        wbd = jnp.concatenate([jnp.concatenate([w1s, zero], axis=2), jnp.concatenate([zero, w1s], axis=2)], axis=1)
        pe8 = jnp.broadcast_to(pe[i].reshape(1, 2 * kd), (SUBLANES, 2 * kd))
        return wbd.astype(BF16), w1i.reshape(2 * kd, CMP_HID).astype(BF16), pe8, w2[i].astype(BF16)

    ck, cv = cmp_w(k_w1, k_w2, k_pe), cmp_w(v_w1, v_w2, v_pe)
    cmp_weights = tuple(jnp.stack([a, b]) for a, b in zip(ck, cv))
    return wq, wkv, cmp_weights


def _from_feature_major(x_t, heads, hd):
    b, _, t = x_t.shape
    return x_t.reshape(b, 2, heads, hd, t).transpose(0, 4, 1, 2, 3)


def _pool_view(cache):
    d, n, pg, two, h, hd = cache.shape
    return cache.transpose(0, 1, 3, 4, 5, 2).reshape(d * n, two * h * hd, pg)


def _stack_rows(x, n_heads, width, col_of_head, k_width):
    b = x.shape[0]
    out = jnp.zeros((b, n_heads, SUBLANES, k_width), x.dtype)
    for h in range(n_heads):
        out = out.at[:, h, :, col_of_head(h):col_of_head(h) + width].set(x[:, :, h * width:(h + 1) * width])
    return out.reshape(b, n_heads * SUBLANES, k_width)


def kernel(x_prompt, x_sample, cache_nsa_cmp, cache_nsa_sel, cache_nsa_win, cache_moba, cache_diff, page_table, p_prompt, p_sample, g_attn, w_in, w_out, nsa_cmp_k_w1, nsa_cmp_k_w2, nsa_cmp_k_pe, nsa_cmp_v_w1, nsa_cmp_v_w2, nsa_cmp_v_pe, diff_lq1, diff_lk1, diff_lq2, diff_lk2, diff_norm_g, g_ffn, ffn_w_gate, ffn_w_up, ffn_w_down, moe_router, moe_w_gate, moe_w_up, moe_w_down, ple_gate, ple_proj, g_ple, g_final):
    depth = w_in.shape[0]
    bp, tp, d = x_prompt.shape
    bs, ts, _ = x_sample.shape
    n_pool, page = cache_nsa_cmp.shape[1], cache_nsa_cmp.shape[2]
    n_pages = page_table.shape[1]
    past_len = n_pages * page
    w_buf = cache_nsa_win.shape[2]
    per_step = 32
    assert ts <= SUBLANES and tp % 512 == 0 and n_pages % per_step == 0 and past_len % MOBA_BLOCK == 0
    pt_flat = page_table.reshape(-1).astype(I32)
    pools = {"cmp": _pool_view(cache_nsa_cmp), "sel": _pool_view(cache_nsa_sel),
             "moba": _pool_view(cache_moba), "diff": _pool_view(cache_diff)}
    win_cache_t = cache_nsa_win.transpose(0, 1, 3, 4, 5, 2).reshape(depth, bs, W_NSA, w_buf)

    tab_p = _rope_tables(jnp.arange(tp, dtype=I32))
    tab_s = _rope_tables(past_len + (jnp.arange(bs * ts, dtype=I32) % ts))

    hp = x_prompt.reshape(bp * tp, d)
    hs = x_sample.reshape(bs * ts, d)
    outs_p = {k: [] for k in ("cmp", "sel", "win", "moba", "diff")}
    outs_s = {k: [] for k in ("cmp", "sel", "win", "moba", "diff")}
    y_p = y_s = None
    n_dense = 0
    for i in range(depth):
        wq, wkv, cmpw = _layer_weights(i, w_in, nsa_cmp_k_w1, nsa_cmp_k_w2, nsa_cmp_k_pe,
                                       nsa_cmp_v_w1, nsa_cmp_v_w2, nsa_cmp_v_pe)
        lam_init = 0.8 - 0.6 * math.exp(-0.3 * i)
        dpar = jnp.zeros((SUBLANES, LANES), F32)
        for r, v in enumerate((diff_lq1, diff_lk1, diff_lq2, diff_lk2)):
            dpar = dpar.at[r, :DH_C].set(v[i].astype(F32))
        diff_par = (dpar, diff_norm_g[i].reshape(1, DV_C))
        moe = i % 2 == 1
        jf = i // 2
        w_g, w_u, w_d = ((moe_w_gate[jf], moe_w_up[jf], moe_w_down[jf]) if moe
                         else (ffn_w_gate[jf], ffn_w_up[jf], ffn_w_down[jf]))
        dff = w_g.shape[-1]
        tf = dff // 2 if (dff // 2) % LANES == 0 else dff

        def blocks(w):
            return w.reshape(w.shape[:-1] + (dff // tf, tf))

        w_gu = jnp.concatenate([blocks(w_g), blocks(w_u)], axis=-1).reshape(w_g.shape[:-1] + (2 * dff,)).astype(BF16)
        ffn = (w_gu, w_d.astype(BF16))
        if moe:
            ffn += (jnp.pad(moe_router[jf].astype(F32), ((0, 0), (0, LANES - moe_router.shape[2]))),)
        wout = w_out[i].astype(BF16)
        pgate, pproj = ple_gate[i].astype(BF16), ple_proj[i].astype(BF16)
        gfin = g_final if i == depth - 1 else None

        zq, cmp_t, sel_t, win_t, moba_t, diff_t = _proj_in(hp, g_attn[i], wq, wkv, tab_p, bp, tp, 512)
        for k, v in zip(("cmp", "sel", "win", "moba", "diff"), (cmp_t, sel_t, win_t, moba_t, diff_t)):
            outs_p[k].append(v)
        zq3 = zq.reshape(bp, tp, NQ)
        n_cmp = (tp - L_CMP) // STRIDE + 1
        kvcmp = _compress_tail(_half_proj(cmp_t, cmpw[0]), *cmpw[1:])
        tq = 512
        o_cmp, bm = _cmp_attn(zq3, kvcmp, n_cmp, tp // L_SEL, 0, tq)
        ta = 512
        qa_blk, qb_blk, qc_blk = (ta, H_A * HD, 0), (ta, H_B * HD, 1), (ta, H_C * DV_C, QC0 // (H_C * DV_C))
        o_sel = _flash("sel", zq3, [sel_t], t0=0, out_w=H_A * HD, stacked=False, q_block=qa_blk, gates=zq3, bm=bm, tile=512)
        o_win = _flash("win", zq3, [win_t], t0=0, out_w=H_A * HD, stacked=False, q_block=qa_blk, gates=zq3, tile=512)
        means = _means(moba_t)
        o_b = _flash("moba", zq3, [moba_t], t0=0, out_w=H_B * HD, stacked=False, q_block=qb_blk, means=means,
                     tile=512, n_blk=tp // MOBA_BLOCK)
        o_c = _flash("diff", zq3, [diff_t], t0=0, out_w=H_C * DV_C, stacked=False, q_block=qc_blk, diff_par=diff_par,
                     tile=512, lam_init=lam_init)
        flat = lambda x: x.reshape(bp * tp, x.shape[-1])
        hp, y = _post(hp, flat(o_cmp), flat(o_sel), flat(o_win), flat(o_b), flat(o_c), wout, g_ffn[i], ffn,
                      g_ple[i], pgate, pproj, p_prompt[i].reshape(bp * tp, -1), gfin, moe, 512, tf)
        y_p = y if y is not None else y_p

        ms = bs * ts
        zq, cmp_n, sel_n, win_n, moba_n, diff_n = _proj_in(hs, g_attn[i], wq, wkv, tab_s, 1, ms, ms)
        news = dict(cmp=cmp_n, sel=sel_n, win=win_n, moba=moba_n, diff=diff_n)
        for k, v in news.items():
            outs_s[k].append(v)

        def tail_of(x):
            w = x.shape[1]
            return jnp.pad(x.reshape(w, bs, ts).transpose(1, 0, 2), ((0, 0), (0, 0), (0, LANES - ts)))

        zq8 = jnp.pad(zq.reshape(bs, ts, NQ), ((0, 0), (0, SUBLANES - ts), (0, 0)))
        l_tot = past_len + ts
        n_cmp = (l_tot - L_CMP) // STRIDE + 1
        assert (n_cmp + 1) * STRIDE == past_len
        half = _half_proj(None, cmpw[0], paged=(pt_flat, pools["cmp"], i * n_pool, n_pages, per_step))
        kvcmp = _compress_tail(half, *cmpw[1:])
        n_sel = -(-l_tot // L_SEL)
        o_cmp, bm = _cmp_attn(zq8, kvcmp, n_cmp, n_sel, past_len, SUBLANES)
        q_sel = _stack_rows(zq8[:, :, QA0:QA0 + H_A * HD], H_A, HD, lambda h: (h // HG_A) * HD, G_A * HD)
        bm_rows = jnp.repeat(bm, HG_A, axis=1).reshape(bs, H_A * SUBLANES, bm.shape[3])
        blk_step = per_step * page // L_SEL
        n_steps = n_pages // per_step
        assert n_sel == n_steps * blk_step + 1 and blk_step <= LANES
        bm_steps = bm_rows[:, :, :n_steps * blk_step].reshape(bs, H_A * SUBLANES, n_steps, blk_step).transpose(0, 2, 1, 3)
        bm_st = jnp.concatenate([jnp.pad(bm_steps, ((0, 0), (0, 0), (0, 0), (0, LANES - blk_step))),
                                 jnp.pad(bm_rows[:, None, :, n_steps * blk_step:n_sel],
                                         ((0, 0), (0, 0), (0, 0), (0, LANES - 1)))], axis=1)
        ga8 = zq8
        o_sel = _flash("sel", q_sel, None, t0=past_len, out_w=H_A * HD, stacked=True,
                       q_block=(H_A * SUBLANES, G_A * HD, 0), gates=ga8, bm=bm_st,
                       paged=(pt_flat, pools["sel"], i * n_pool, n_pages, per_step), tail=tail_of(sel_n))
        qa8_blk = (SUBLANES, H_A * HD, 0)
        o_win = _flash("win", zq8, [win_cache_t[i], tail_of(win_n)], t0=past_len, out_w=H_A * HD, stacked=False,
                       q_block=qa8_blk, gates=ga8, kv_k0=[past_len - w_buf, past_len])
        means = _means(None, paged=(pt_flat, pools["moba"], i * n_pool, n_pages, per_step))
        q_moba = _stack_rows(zq8[:, :, QB0:QB0 + H_B * HD], H_B, HD, lambda h: h * HD, H_B * HD)
        o_b = _flash("moba", q_moba, None, t0=past_len, out_w=H_B * HD, stacked=True,
                     q_block=(H_B * SUBLANES, H_B * HD, 0), means=means,
                     paged=(pt_flat, pools["moba"], i * n_pool, n_pages, per_step), tail=tail_of(moba_n),
                     n_blk=-(-l_tot // MOBA_BLOCK))
        q_diff = _stack_rows(zq8[:, :, QC0:QC0 + H_C * DV_C], 2 * H_C, DH_C, lambda u: u * DH_C, H_C * DV_C)
        o_c = _flash("diff", q_diff, None, t0=past_len, out_w=H_C * DV_C, stacked=True,
                     q_block=(2 * H_C * SUBLANES, H_C * DV_C, 0), diff_par=diff_par,
                     paged=(pt_flat, pools["diff"], i * n_pool, n_pages, per_step), tail=tail_of(diff_n),
                     lam_init=lam_init)
        flat = lambda x: x[:, :ts].reshape(ms, x.shape[-1])
        hs, y = _post(hs, flat(o_cmp), flat(o_sel), flat(o_win), flat(o_b), flat(o_c), wout, g_ffn[i], ffn,
                      g_ple[i], pgate, pproj, p_sample[i].reshape(ms, -1), gfin, moe, ms, tf)
        y_s = y if y is not None else y_s

    heads = dict(cmp=(G_A, HD), sel=(G_A, HD), win=(G_A, HD), moba=(H_B, HD), diff=(H_C, DV_C))
    res_p, res_s = {}, {}
    for k, (nh, hd) in heads.items():
        res_p[k] = jnp.stack([_from_feature_major(x, nh, hd) for x in outs_p[k]], axis=0)
        res_s[k] = jnp.stack([x[0].T.reshape(bs, ts, 2, nh, hd) for x in outs_s[k]], axis=0)
    keep = min(WINDOW, tp)
    win_p = res_p["win"][:, :, tp - keep:]
    win_all = jnp.concatenate([cache_nsa_win, res_s["win"]], axis=2)
    win_s = win_all[:, :, win_all.shape[2] - min(WINDOW, win_all.shape[2]):]
    return (y_p.reshape(bp, tp, d), y_s.reshape(bs, ts, d), res_p["cmp"], res_p["sel"], win_p, res_p["moba"],
            res_p["diff"], res_s["cmp"], res_s["sel"], win_s, res_s["moba"], res_s["diff"])
```

```python
import functools
import math

import jax
import jax.numpy as jnp
from jax import lax
from jax.experimental import pallas as pl
from jax.experimental.pallas import tpu as pltpu

F32 = jnp.float32
BF16 = jnp.bfloat16
I32 = jnp.int32

HD = 64
H_A = 6
G_A = 2
HG_A = H_A // G_A
H_B = 6
H_C = 4
DH_C = 32
DV_C = 2 * DH_C
ROPE_THETA = 500000.0
L_CMP = 32
STRIDE = 16
L_SEL = 64
SEL_RATIO = L_SEL // STRIDE
N_SEL_TOP = 16
N_LOCAL = 2
WINDOW = 512
CMP_HID = 2 * HD
MOBA_BLOCK = 256
MOBA_TOPK = 3
TOP_K = 2
EPS = 1e-6
BIG = 1e9
NEG = -1e30
LOG2E = math.log2(math.e)

LANES = 128
SUBLANES = 8
VMEM_LIMIT_BYTES = 56 * 1024 * 1024

QA0, QB0, QC0, GA0 = 0, H_A * HD, (H_A + H_B) * HD, (H_A + H_B) * HD + H_C * DV_C
NQ = GA0 + LANES
W_NSA = 2 * G_A * HD
W_MOBA = 2 * H_B * HD
W_DIFF = 2 * H_C * DV_C
CMP0, SEL0, WIN0, MOBA0 = 0, W_NSA, 2 * W_NSA, 3 * W_NSA
DIFF0 = MOBA0 + W_MOBA
NKV = DIFF0 + W_DIFF

HIGHEST = lax.Precision.HIGHEST
NT = (((1,), (1,)), ((), ()))


def _cparams(n_grid):
    return pltpu.CompilerParams(dimension_semantics=("arbitrary",) * n_grid,
                                vmem_limit_bytes=VMEM_LIMIT_BYTES)


def _dot(a, b):
    return jnp.dot(a, b, preferred_element_type=F32)


def _dot_nt(a, b):
    return lax.dot_general(a, b, NT, preferred_element_type=F32)


def _sigmoid(x):
    return 1.0 / (1.0 + jnp.exp(-x))


def _rms(x, g):
    return (x * lax.rsqrt(jnp.mean(x * x, axis=-1, keepdims=True) + EPS)) * g


def _topk_mask(score, k, n_valid):
    lane = lax.broadcasted_iota(I32, score.shape, 1)
    rank = jnp.zeros(score.shape, I32)
    for i in range(n_valid):
        col = jnp.broadcast_to(score[:, i:i + 1], score.shape)
        beats = (col > score) | ((col == score) & (lane > i))
        rank = rank + beats.astype(I32)
    return (rank < k) & (lane < n_valid)


def _proj_in_kernel(x_ref, g_ref, wq_ref, wkv_ref, c64_ref, sa64_ref, sb64_ref, c32_ref, sa32_ref, sb32_ref,
                    ck_ref, sk_ref, ck32_ref, sk32_ref, zq_ref, cmp_ref, sel_ref, win_ref, moba_ref, diff_ref):
    ab = _rms(x_ref[...], g_ref[...]).astype(BF16)

    def rope_q(z, c, sa, sb, half):
        return z * c + pltpu.roll(z, half, 1) * sa + pltpu.roll(z, LANES - half, 1) * sb

    for c0, width, kind in ((QA0, H_A * HD, 64), (QB0, H_B * HD, 64), (QC0, H_C * DV_C, 32), (GA0, LANES, 0)):
        z = _dot(ab, wq_ref[:, c0:c0 + width])
        for k in range(width // LANES):
            zc = z[:, k * LANES:(k + 1) * LANES]
            if kind == 64:
                zc = rope_q(zc, c64_ref[...], sa64_ref[...], sb64_ref[...], 8)
            elif kind == 32:
                zc = rope_q(zc, c32_ref[...], sa32_ref[...], sb32_ref[...], 4)
            else:
                zc = _sigmoid(zc)
            zq_ref[:, c0 + k * LANES:c0 + (k + 1) * LANES] = zc

    ck, sk, ck32, sk32 = ck_ref[...], sk_ref[...], ck32_ref[...], sk32_ref[...]

    def rope_k64(zh):
        x1, x2 = zh[0:8], zh[8:16]
        return jnp.concatenate([x1 * ck - x2 * sk, x2 * ck + x1 * sk, zh[16:HD]], axis=0)

    def rope_k32(zh):
        x = zh[0:8]
        return jnp.concatenate([x * ck32 + pltpu.roll(x, 4, 0) * sk32, zh[8:DH_C]], axis=0)

    for r0, width, o_ref, unit, n_rot in ((CMP0, W_NSA, cmp_ref, HD, G_A), (SEL0, W_NSA, sel_ref, HD, G_A),
                                          (WIN0, W_NSA, win_ref, HD, G_A), (MOBA0, W_MOBA, moba_ref, HD, H_B),
                                          (DIFF0, W_DIFF, diff_ref, DH_C, 2 * H_C)):
        z = _dot_nt(wkv_ref[r0:r0 + width, :], ab)
        parts = []
        for u in range(n_rot):
            zh = z[u * unit:(u + 1) * unit]
            parts.append(rope_k64(zh) if unit == HD else rope_k32(zh))
        parts.append(z[n_rot * unit:width])
        o_ref[0] = jnp.concatenate(parts, axis=0)


def _rope_tables(pos):
    posf = pos.astype(F32)[:, None]

    def ang(rot):
        half = rot // 2
        inv = ROPE_THETA ** (-2.0 * jnp.arange(half, dtype=F32) / rot)
        a = posf * inv
        return jnp.cos(a), jnp.sin(a)

    def token_major(period, cos, sin):
        half = cos.shape[1]
        lane = jnp.arange(LANES) % period
        idx = jnp.where(lane < half, lane, lane - half) % half
        lo, hi = lane < half, (lane >= half) & (lane < 2 * half)
        c = jnp.where(lo | hi, cos[:, idx], 1.0)
        sa = jnp.where(hi, sin[:, idx], 0.0)
        sb = jnp.where(lo, -sin[:, idx], 0.0)
        return c, sa, sb

    cos64, sin64 = ang(HD // 4)
    cos32, sin32 = ang(DH_C // 4)
    t64 = token_major(HD, cos64, sin64)
    t32 = token_major(DH_C, cos32, sin32)
    ck, sk = cos64.T, sin64.T
    ck32 = jnp.concatenate([cos32, cos32], axis=1).T
    sk32 = jnp.concatenate([-sin32, sin32], axis=1).T
    return t64 + t32 + (ck, sk, ck32, sk32)


def _proj_in(h, g, wq, wkv, tables, n_seq, t_seq, tm):
    m, d = h.shape
    per = t_seq // tm
    row = lambda i: (i, 0)
    tab = lambda i: (i % per, 0)
    tabk = lambda i: (0, i % per)
    fm = lambda i: (i // per, 0, i % per)
    full = lambda i: (0, 0)
    in_specs = [pl.BlockSpec((tm, d), row), pl.BlockSpec((1, d), full),
                pl.BlockSpec((d, NQ), full), pl.BlockSpec((NKV, d), full)]
    in_specs += [pl.BlockSpec((tm, LANES), tab)] * 6 + [pl.BlockSpec((SUBLANES, tm), tabk)] * 4
    widths = (W_NSA, W_NSA, W_NSA, W_MOBA, W_DIFF)
    out_specs = [pl.BlockSpec((tm, NQ), row)] + [pl.BlockSpec((1, w, tm), fm) for w in widths]
    out_shape = [jax.ShapeDtypeStruct((m, NQ), F32)] + [jax.ShapeDtypeStruct((n_seq, w, t_seq), F32) for w in widths]
    return pl.pallas_call(_proj_in_kernel, grid=(m // tm,), in_specs=in_specs, out_specs=out_specs,
                          out_shape=out_shape, compiler_params=_cparams(1), name="proj_in")(
        h, g.reshape(1, d), wq, wkv, *tables)


def _half_proj_kernel(cfg, *refs):
    refs = list(refs)
    if cfg["paged"]:
        refs.pop(0)
    n_kv = cfg["n_kv"]
    kv_refs, w_ref, o_ref, rows_s = refs[:n_kv], refs[n_kv], refs[n_kv + 1], refs[n_kv + 2]
    chunks = [(r, c) for r in kv_refs for c in range(r.shape[2] // LANES)]
    n_half = len(chunks) * LANES // STRIDE
    for kv in range(2):
        for n, (r, c) in enumerate(chunks):
            rows_s[n * LANES:(n + 1) * LANES, :] = r[0, kv * G_A * HD:(kv + 1) * G_A * HD, c * LANES:(c + 1) * LANES].T
        acc = jnp.zeros((n_half, 2 * G_A * CMP_HID), F32)
        for s in range(STRIDE):
            acc = acc + _dot(rows_s[pl.ds(s, n_half, stride=STRIDE), :].astype(BF16), w_ref[kv, s])
        o_ref[0, kv] = acc


def _half_proj(kvs, wbd, paged=None):
    if paged is not None:
        page_table, pool, row0, n_pages, per_step = paged
        b = page_table.shape[0] // n_pages
        w, pg = pool.shape[1], pool.shape[2]
        in_specs = [pl.BlockSpec((1, w, pg), lambda bi, j, pt, n=n: (row0 + pt[bi * n_pages + j * per_step + n], 0, 0))
                    for n in range(per_step)]
        operands = [page_table] + [pool] * per_step
        n_kv, grid, npre, step_tokens, total = per_step, (b, n_pages // per_step), 1, per_step * pg, n_pages * pg
    else:
        x = kvs
        b, w, total = x.shape
        in_specs = [pl.BlockSpec((1, w, total), lambda bi, j: (bi, 0, 0))]
        operands = [x]
        n_kv, grid, npre, step_tokens = 1, (b, 1), 0, total
    in_specs.append(pl.BlockSpec(wbd.shape, lambda bi, j, *_: (0, 0, 0, 0)))
    wo = 2 * G_A * CMP_HID
    grid_spec = pltpu.PrefetchScalarGridSpec(
        num_scalar_prefetch=npre, grid=grid, in_specs=in_specs,
        out_specs=pl.BlockSpec((1, 2, step_tokens // STRIDE, wo), lambda bi, j, *_: (bi, 0, j, 0)),
        scratch_shapes=[pltpu.VMEM((step_tokens, G_A * HD), F32)])
    return pl.pallas_call(functools.partial(_half_proj_kernel, dict(paged=paged is not None, n_kv=n_kv)),
                          grid_spec=grid_spec, out_shape=jax.ShapeDtypeStruct((b, 2, total // STRIDE, wo), F32),
                          compiler_params=_cparams(2), name="nsa_half_proj")(*operands, wbd)


def _compress_tail_kernel(a_ref, w1f_ref, pe_ref, w2_ref, o_ref):
    nh = a_ref.shape[2]
    a = a_ref[0, 0]
    pe_term = _dot(pe_ref[0].astype(BF16), w1f_ref[0])[0:1]
    hid = a[:, :CMP_HID] + pltpu.roll(a[:, CMP_HID:], nh - 1, 0) + pe_term
    act = hid * _sigmoid(hid)
    o_ref[0, 0, 0] = _dot(act.astype(BF16), w2_ref[0])


def _compress_tail(a, w1flat, pe8, w2):
    b, _, nh, _ = a.shape
    kd = STRIDE * HD
    return pl.pallas_call(
        _compress_tail_kernel, grid=(b, 2, G_A),
        in_specs=[pl.BlockSpec((1, 1, nh, 2 * CMP_HID), lambda bi, kv, gi: (bi, kv, 0, gi)),
                  pl.BlockSpec((1, 2 * kd, CMP_HID), lambda bi, kv, gi: (kv, 0, 0)),
                  pl.BlockSpec((1, SUBLANES, 2 * kd), lambda bi, kv, gi: (kv, 0, 0)),
                  pl.BlockSpec((1, CMP_HID, HD), lambda bi, kv, gi: (kv, 0, 0))],
        out_specs=pl.BlockSpec((1, 1, 1, nh, HD), lambda bi, kv, gi: (bi, kv, gi, 0, 0)),
        out_shape=jax.ShapeDtypeStruct((b, 2, G_A, nh, HD), F32),
        compiler_params=_cparams(3), name="nsa_compress")(a, w1flat, pe8, w2)


def _cmp_attn_kernel(cfg, q_ref, ga_ref, kv_ref, o_ref, bm_ref):
    tq, n_cmp, n_sel, t0 = cfg["tq"], cfg["n_cmp"], cfg["n_sel"], cfg["t0"]
    ncp, nsp = kv_ref.shape[3], bm_ref.shape[3]
    i = pl.program_id(1)
    qpos = t0 + i * tq + lax.broadcasted_iota(I32, (tq, 1), 0)
    q = q_ref[0]
    gates = ga_ref[0]
    n_idx = lax.broadcasted_iota(I32, (1, ncp), 1)
    valid = ((n_idx * STRIDE + (L_CMP - 1)) <= qpos) & (n_idx < n_cmp)
    nn = lax.broadcasted_iota(I32, (ncp, nsp), 0)
    jj = lax.broadcasted_iota(I32, (ncp, nsp), 1)
    blk = lax.shift_right_logical(nn, 2)
    overlap = ((blk == jj) | (((nn & 3) == 3) & (blk + 1 == jj))) & (nn < n_cmp)
    mmap = overlap.astype(F32)
    lane = lax.broadcasted_iota(I32, (tq, nsp), 1)
    qblk = lax.shift_right_logical(qpos, 6)
    causal = lane <= qblk
    forced = (lane == 0) | (lane > qblk - N_LOCAL)
    scale = HD ** -0.5
    for g in range(G_A):
        kc = kv_ref[0, 0, g]
        vc = kv_ref[0, 1, g].astype(BF16)
        imp = jnp.zeros((tq, ncp), F32)
        for z in range(HG_A):
            hcol = (g * HG_A + z) * HD
            s = lax.dot_general(q[:, hcol:hcol + HD], kc, NT, precision=HIGHEST, preferred_element_type=F32) * scale
            s = jnp.where(valid, s, NEG)
            e = jnp.where(valid, jnp.exp(s - jnp.max(s, axis=-1, keepdims=True)), 0.0)
            p = e / jnp.maximum(jnp.sum(e, axis=-1, keepdims=True), 1e-30)
            imp = imp + p
            gcol = g * HG_A + z
            o_ref[0, :, hcol:hcol + HD] = gates[:, gcol:gcol + 1] * _dot(p.astype(BF16), vc)
        imp_blk = jnp.dot(imp, mmap, precision=HIGHEST, preferred_element_type=F32)
        k_top = min(N_SEL_TOP, n_sel)
        if tq % LANES == 0:
            nb8 = -(-n_sel // SUBLANES) * SUBLANES
            imp_t = imp_blk.T[0:nb8]
            blk_t = lax.broadcasted_iota(I32, (nb8, tq), 0)
            qblk_t = lax.shift_right_logical(t0 + i * tq + lax.broadcasted_iota(I32, (1, tq), 1), 6)
            causal_t = blk_t <= qblk_t
            forced_t = (blk_t == 0) | (blk_t > qblk_t - N_LOCAL)
            score_t = jnp.where(causal_t, jnp.where(forced_t, BIG, imp_t), -BIG)
            rank = jnp.zeros((nb8, tq), I32)
            for ib in range(n_sel):
                row = score_t[ib:ib + 1, :]
                rank = rank + ((row > score_t) | ((row == score_t) & (blk_t > ib))).astype(I32)
            picked_t = ((rank < k_top) & causal_t & (blk_t < n_sel)).astype(F32)
            bm_ref[0, g] = jnp.concatenate([picked_t, jnp.zeros((nsp - nb8, tq), F32)], axis=0).T
        else:
            score = jnp.where(causal, jnp.where(forced, BIG, imp_blk), -BIG)
            picked = _topk_mask(score, k_top, n_sel) & causal
            bm_ref[0, g] = picked.astype(F32)


def _cmp_attn(zq3, kvcmp, n_cmp, n_sel, t0, tq):
    b, t, _ = zq3.shape
    ncp = kvcmp.shape[3]
    nsp = -(-n_sel // LANES) * LANES
    cfg = dict(tq=tq, n_cmp=n_cmp, n_sel=n_sel, t0=t0)
    return pl.pallas_call(
        functools.partial(_cmp_attn_kernel, cfg), grid=(b, t // tq),
        in_specs=[pl.BlockSpec((1, tq, H_A * HD), lambda bi, i: (bi, i, QA0 // (H_A * HD))),
                  pl.BlockSpec((1, tq, LANES), lambda bi, i: (bi, i, GA0 // LANES)),
                  pl.BlockSpec((1, 2, G_A, ncp, HD), lambda bi, i: (bi, 0, 0, 0, 0))],
        out_specs=[pl.BlockSpec((1, tq, H_A * HD), lambda bi, i: (bi, i, 0)),
                   pl.BlockSpec((1, G_A, tq, nsp), lambda bi, i: (bi, 0, i, 0))],
        out_shape=[jax.ShapeDtypeStruct((b, t, H_A * HD), F32), jax.ShapeDtypeStruct((b, G_A, t, nsp), F32)],
        compiler_params=_cparams(2), name="nsa_cmp_attn")(zq3, zq3, kvcmp)


def _means_kernel(cfg, *refs):
    paged, n_kv = cfg["paged"], cfg["n_kv"]
    refs = list(refs)
    if paged:
        refs.pop(0)
    kv_refs, o_ref, acc_s = refs[:n_kv], refs[n_kv], refs[n_kv + 1]
    j = pl.program_id(1)

    @pl.when(j == 0)
    def _():
        acc_s[...] = jnp.zeros(acc_s.shape, F32)

    chunks_per_block = MOBA_BLOCK // LANES
    lane = lax.broadcasted_iota(I32, acc_s.shape, 1)
    acc = acc_s[...]
    chunk0 = j * cfg["chunks_per_step"]
    for n, r in enumerate(kv_refs):
        x = r[0]
        for c in range(x.shape[1] // LANES):
            blk = (chunk0 + n * (x.shape[1] // LANES) + c) // chunks_per_block
            s = jnp.sum(x[:, c * LANES:(c + 1) * LANES], axis=1, keepdims=True)
            acc = acc + jnp.where(lane == blk, s, 0.0)
    acc_s[...] = acc

    @pl.when(j == pl.num_programs(1) - 1)
    def _():
        o_ref[0] = (acc_s[...] / float(MOBA_BLOCK)).T


def _flash_kernel(cfg, *refs):
    kind, jobs, rows = cfg["kind"], cfg["jobs"], cfg["rows"]
    stacked, t0 = cfg["stacked"], cfg["t0"]
    it = iter(refs)
    if cfg["paged"]:
        next(it)
    q_ref = next(it)
    ga_ref = next(it) if kind in ("sel", "win") else None
    bm_ref = next(it) if kind == "sel" else None
    mt_ref = next(it) if kind == "moba" else None
    dp_ref = next(it) if kind == "diff" else None
    ng_ref = next(it) if kind == "diff" else None
    kv_refs = [next(it) for _ in range(cfg["n_kv"])]
    tail_ref = next(it) if cfg["tail"] else None
    tail_bm_ref = next(it) if (kind == "sel" and cfg["tail"]) else None
    o_ref = next(it)
    m_s, l_s, acc_s = next(it), next(it), next(it)
    bm_s = next(it) if kind == "moba" else None
    kvb_s = next(it) if cfg["chunked"] else None

    i, j = pl.program_id(1), pl.program_id(2)
    nk = pl.num_programs(2)
    qscale = cfg["scale"] * LOG2E
    n_blk = cfg["n_blk"]

    def positions(r0, n, axis):
        rid = r0 + lax.broadcasted_iota(I32, (n, 1) if axis == 0 else (1, n), axis)
        return t0 + (rid & (SUBLANES - 1)) if stacked else t0 + i * rows + rid

    @pl.when(j == 0)
    def _init():
        m_s[...] = jnp.full(m_s.shape, NEG, F32)
        l_s[...] = jnp.zeros(l_s.shape, F32)
        acc_s[...] = jnp.zeros(acc_s.shape, F32)
        if kind == "moba":
            q = q_ref[0]
            k_top = min(MOBA_TOPK, n_blk)
            if stacked:
                lane = lax.broadcasted_iota(I32, (rows, LANES), 1)
                own = lax.shift_right_logical(positions(0, rows, 0), 8)
                past = lane < own
                gate = lax.dot_general(q, mt_ref[0], NT, precision=HIGHEST, preferred_element_type=F32)
                picked = _topk_mask(jnp.where(past, gate, -BIG), k_top, n_blk) & past
                bm_s[0] = (picked | (lane == own)).astype(F32)
            else:
                nb8 = -(-n_blk // SUBLANES) * SUBLANES
                blk = lax.broadcasted_iota(I32, (nb8, rows), 0)
                own = lax.shift_right_logical(positions(0, rows, 1), 8)
                past = blk < own
                for mi, (qc0, qw, kr0, kw) in enumerate(cfg["mask_jobs"]):
                    gate = lax.dot_general(mt_ref[0, 0:nb8, kr0:kr0 + kw], q[:, qc0:qc0 + qw], NT,
                                           precision=HIGHEST, preferred_element_type=F32)
                    score = jnp.where(past, gate, -BIG)
                    rank = jnp.zeros((nb8, rows), I32)
                    for ib in range(n_blk):
                        row = score[ib:ib + 1, :]
                        rank = rank + ((row > score) | ((row == score) & (blk > ib))).astype(I32)
                    picked = ((rank < k_top) & past) | (blk == own)
                    full = jnp.concatenate([picked.astype(F32), jnp.zeros((LANES - nb8, rows), F32)], axis=0)
                    bm_s[mi] = full.T

    def update(ji, rsl, qj, tiles, vw):
        ss = []
        for kt, _, bias in tiles:
            s = _dot(qj, kt)
            ss.append(s if bias is None else s + bias)
        m_cur = lane_fold(ss, jnp.maximum)
        m_cur = jnp.broadcast_to(jnp.max(m_cur, axis=-1, keepdims=True), m_cur.shape)
        m_prev = m_s[ji, rsl, :]
        m_new = jnp.maximum(m_prev, m_cur)
        alpha = jnp.exp2(m_prev - m_new)
        ps = [jnp.exp2(s - lanes_to(m_new, s.shape[1])) for s in ss]
        tot = lane_fold(ps, jnp.add)
        tot = jnp.broadcast_to(jnp.sum(tot, axis=-1, keepdims=True), tot.shape)
        l_s[ji, rsl, :] = alpha * l_s[ji, rsl, :] + tot
        m_s[ji, rsl, :] = m_new
        pv = _dot_nt(ps[0].astype(BF16), tiles[0][1])
        for p, (_, vt, _) in zip(ps[1:], tiles[1:]):
            pv = pv + _dot_nt(p.astype(BF16), vt)
        acc_s[ji, rsl, 0:vw] = lanes_to(alpha, vw) * acc_s[ji, rsl, 0:vw] + pv

    def lane_fold(xs, op):
        out = None
        for x in xs:
            for c in range(x.shape[1] // LANES):
                part = x[:, c * LANES:(c + 1) * LANES]
                out = part if out is None else op(out, part)
        return out

    def lanes_to(x, width):
        if width <= LANES:
            return x[:, 0:width]
        reps = -(-width // LANES)
        return jnp.concatenate([x] * reps, axis=1)[:, 0:width]

    def causal_bias(qp, k0, tks):
        kpos = k0 + lax.broadcasted_iota(I32, (1, tks), 1)
        ok = kpos <= qp
        if kind == "win":
            ok = ok & (kpos >= qp - WINDOW)
        return jnp.where(ok, 0.0, NEG)

    def chunked_tile(k0):
        tk = kvb_s.shape[1]
        kvb_s[...] = kv_refs[0][0].astype(BF16)
        blocked = None
        if kind in ("sel", "moba"):
            shift = 6 if kind == "sel" else 8
            kb = lax.shift_right_logical(k0 + lax.broadcasted_iota(I32, (LANES, tk), 1), shift)
            blocked = jnp.where(lax.broadcasted_iota(I32, (LANES, tk), 0) == kb, NEG, 0.0).astype(BF16)

        n_chunk = rows // cfg["row_chunk"]

        def chunk(c, carry):
            if n_chunk == 1:
                r0, rsl = 0, slice(None)
            else:
                r0 = pl.multiple_of(c * cfg["row_chunk"], cfg["row_chunk"])
                rsl = pl.ds(r0, cfg["row_chunk"])
            qc = (q_ref[0, rsl, :] * qscale).astype(BF16)
            base = causal_bias(positions(r0, cfg["row_chunk"], 0), k0, tk)
            unpicked = {}
            for ji, (qc0, qw, kr0, kw, vr0, vw, mi) in enumerate(jobs):
                qj, kt = qc[:, qc0:qc0 + qw], kvb_s[kr0:kr0 + kw, :]
                if blocked is not None:
                    if mi not in unpicked:
                        bm = bm_ref[0, mi, rsl, :] if kind == "sel" else bm_s[mi, rsl, :]
                        unpicked[mi] = (1.0 - bm).astype(BF16)
                    qj = jnp.concatenate([qj, unpicked[mi]], axis=1)
                    kt = jnp.concatenate([kt, blocked], axis=0)
                update(ji, rsl, qj, [(kt, kvb_s[vr0:vr0 + vw, :], base)], vw)
            return carry

        if n_chunk == 1:
            chunk(0, 0)
        else:
            lax.fori_loop(0, n_chunk, chunk, 0)

    def column_bias(col):
        return (col - 1.0) * (-NEG)

    def whole_tiles(pairs, page_cols=None, tail_col=None):
        qb = (q_ref[0] * qscale).astype(BF16)
        all_rows = slice(None)
        qp = positions(0, rows, 0)
        kvbs = [kv.astype(BF16) for kv, _ in pairs]
        for ji, (qc0, qw, kr0, kw, vr0, vw, mi) in enumerate(jobs):
            tiles = []
            for n, ((kv, k0), kvb) in enumerate(zip(pairs, kvbs)):
                bias = None if k0 is None else causal_bias(qp, k0, kv.shape[1])
                extra = page_cols[n] if page_cols is not None else tail_col
                if extra is not None:
                    bias = extra if bias is None else bias + extra
                tiles.append((kvb[kr0:kr0 + kw], kvb[vr0:vr0 + vw], bias))
            update(ji, all_rows, qb[:, qc0:qc0 + qw], tiles, vw)

    def normalized(ji, vw):
        return acc_s[ji, :, 0:vw] / jnp.maximum(lanes_to(l_s[ji], vw), 1e-30)

    def finalize():
        if kind in ("sel", "win"):
            gates = ga_ref[0]
            branch = 1 if kind == "sel" else 2
            for h in range(H_A):
                gcol = branch * H_A + h
                if stacked:
                    o = normalized(0, cfg["jobs"][0][5])[h * SUBLANES:(h + 1) * SUBLANES,
                                                         (h // HG_A) * HD:(h // HG_A + 1) * HD]
                else:
                    o = normalized(h, HD)
                o_ref[0, :, h * HD:(h + 1) * HD] = gates[:, gcol:gcol + 1] * o
        elif kind == "moba":
            for h in range(H_B):
                if stacked:
                    o = normalized(0, H_B * HD)[h * SUBLANES:(h + 1) * SUBLANES, h * HD:(h + 1) * HD]
                else:
                    o = normalized(h, HD)
                o_ref[0, :, h * HD:(h + 1) * HD] = o
        else:
            dp = dp_ref[...]
            lam_init = cfg["lam_init"]
            lam = (jnp.exp(jnp.sum(dp[0:1] * dp[1:2], axis=-1, keepdims=True))
                   - jnp.exp(jnp.sum(dp[2:3] * dp[3:4], axis=-1, keepdims=True)) + lam_init)
            ng = ng_ref[...]
            full = normalized(0, H_C * DV_C) if stacked else None
            for h in range(H_C):
                if stacked:
                    o1 = full[(2 * h) * SUBLANES:(2 * h + 1) * SUBLANES, h * DV_C:(h + 1) * DV_C]
                    o2 = full[(2 * h + 1) * SUBLANES:(2 * h + 2) * SUBLANES, h * DV_C:(h + 1) * DV_C]
                else:
                    o1, o2 = normalized(2 * h, DV_C), normalized(2 * h + 1, DV_C)
                o_ref[0, :, h * DV_C:(h + 1) * DV_C] = _rms(o1 - lam * o2, ng) * (1.0 - lam_init)

    if cfg["paged"]:
        n_kv = cfg["n_kv"]
        page_cols = None
        if kind == "sel":
            bm = bm_ref[0, 0]
            half = lax.broadcasted_iota(I32, (rows, LANES), 1) < L_SEL
            page_cols = [column_bias(jnp.where(half, bm[:, 2 * n:2 * n + 1], bm[:, 2 * n + 1:2 * n + 2]))
                         for n in range(n_kv)]
        elif kind == "moba":
            per_blk = MOBA_BLOCK // kv_refs[0].shape[2]
            bm = pltpu.roll(bm_s[0], (LANES - (n_kv // per_blk) * j) % LANES, 1)
            page_cols = [column_bias(bm[:, n // per_blk:n // per_blk + 1]) for n in range(n_kv)]
        whole_tiles([(r[0], None) for r in kv_refs], page_cols=page_cols)

        @pl.when(j == nk - 1)
        def _last():
            if tail_ref is not None:
                tail_col = None
                if kind == "sel":
                    tail_col = column_bias(tail_bm_ref[0, 0][:, 0:1])
                elif kind == "moba":
                    tail_col = column_bias(bm_s[0][:, n_blk - 1:n_blk])
                whole_tiles([(tail_ref[0], cfg["tail_k0"])], tail_col=tail_col)
            finalize()
    else:
        k0s = cfg["kv_k0"]
        if k0s is None:
            tk = kv_refs[0].shape[2]
            active = j * tk <= (t0 - cfg["k_base"]) + i * rows + rows - 1
            if kind == "win":
                active = active & ((j + 1) * tk - 1 >= (t0 - cfg["k_base"]) + i * rows - WINDOW)

            @pl.when(active)
            def _tile():
                chunked_tile(cfg["k_base"] + j * tk)
        else:
            whole_tiles([(r[0], k0) for r, k0 in zip(kv_refs, k0s)])

        @pl.when(j == nk - 1)
        def _last():
            finalize()


def _head_jobs(kind):
    if kind in ("sel", "win"):
        return [((g * HG_A + z) * HD, HD, g * HD, HD, G_A * HD + g * HD, HD, g) for g in range(G_A) for z in range(HG_A)]
    if kind == "moba":
        return [(h * HD, HD, h * HD, HD, H_B * HD + h * HD, HD, h) for h in range(H_B)]
    return [(h * DV_C + s * DH_C, DH_C, h * DV_C + s * DH_C, DH_C, H_C * DV_C + h * DV_C, DV_C, 0)
            for h in range(H_C) for s in range(2)]


def _stacked_jobs(kind):
    kc = {"sel": G_A * HD, "moba": H_B * HD, "diff": H_C * DV_C}[kind]
    return [(0, kc, 0, kc, kc, kc, 0)]


def _flash(kind, q, kvs, *, t0, out_w, stacked, q_block, gates=None, bm=None, means=None, diff_par=None,
           tile=None, kv_k0=None, k_base=0, paged=None, tail=None, n_blk=None, lam_init=None, row_chunk=None):
    b = q.shape[0]
    rows, qw, qblk = q_block
    nq = q.shape[1] // rows
    jobs = _stacked_jobs(kind) if stacked else _head_jobs(kind)
    vmax = max(jb[5] for jb in jobs)
    cfg = dict(kind=kind, jobs=jobs, rows=rows, stacked=stacked, t0=t0, paged=paged is not None,
               tail=tail is not None, kv_k0=kv_k0, k_base=k_base, n_blk=n_blk, lam_init=lam_init,
               chunked=tile is not None, row_chunk=row_chunk or rows,
               scale=(DH_C if kind == "diff" else HD) ** -0.5, n_mask=1 if stacked else (G_A if kind == "sel" else H_B))
    if kind == "moba":
        cfg["mask_jobs"] = [(jb[0], jb[1], jb[2], jb[3]) for jb in jobs]
    operands, in_specs = [], []
    npre = 0
    if paged is not None:
        page_table, pool, row0, n_pages, per_step = paged
        npre = 1
        nk = n_pages // per_step
        cfg["n_kv"] = per_step
        cfg["tail_k0"] = n_pages * pool.shape[2]
    elif tile is not None:
        nk = kvs[0].shape[2] // tile
        cfg["n_kv"] = 1
    else:
        nk = 1
        cfg["n_kv"] = len(kvs)

    def add(x, block, imap):
        operands.append(x)
        in_specs.append(pl.BlockSpec(block, imap))

    out_rows = SUBLANES if stacked else rows
    add(q, (1, rows, qw), lambda bi, i, j, *_: (bi, i, qblk))
    if kind in ("sel", "win"):
        add(gates, (1, out_rows, LANES), lambda bi, i, j, *_: (bi, i, gates.shape[2] // LANES - 1))
    if kind == "sel" and paged is not None:
        add(bm, (1, 1, rows, LANES), lambda bi, i, j, *_: (bi, j, 0, 0))
    elif kind == "sel":
        add(bm, (1, bm.shape[1], rows, bm.shape[3]), lambda bi, i, j, *_: (bi, 0, i, 0))
    if kind == "moba":
        add(means, (1,) + means.shape[1:], lambda bi, i, j, *_: (bi, 0, 0))
    if kind == "diff":
        add(diff_par[0], diff_par[0].shape, lambda bi, i, j, *_: (0, 0))
        add(diff_par[1], diff_par[1].shape, lambda bi, i, j, *_: (0, 0))
    if paged is not None:
        w, pg = pool.shape[1], pool.shape[2]
        for n in range(per_step):
            add(pool, (1, w, pg),
                lambda bi, i, j, pt, n=n: (row0 + pt[bi * n_pages + j * per_step + n], 0, 0))
    elif tile is not None:
        w = kvs[0].shape[1]

        def kv_map(bi, i, j, *_):
            hi = (t0 - k_base + i * rows + rows - 1) // tile
            lo = jnp.maximum(t0 - k_base + i * rows - WINDOW, 0) // tile if kind == "win" else 0
            return (bi, 0, jnp.clip(j, lo, hi))

        add(kvs[0], (1, w, tile), kv_map)
    else:
        for x in kvs:
            add(x, (1,) + x.shape[1:], lambda bi, i, j, *_: (bi, 0, 0))
    if tail is not None:
        add(tail, (1,) + tail.shape[1:], lambda bi, i, j, *_: (bi, 0, 0))
        if kind == "sel":
            add(bm, (1, 1, rows, LANES), lambda bi, i, j, *_: (bi, nk, 0, 0))
    scratch = [pltpu.VMEM((len(jobs), rows, LANES), F32), pltpu.VMEM((len(jobs), rows, LANES), F32),
               pltpu.VMEM((len(jobs), rows, vmax), F32)]
    if kind == "moba":
        scratch.append(pltpu.VMEM((cfg["n_mask"], rows, LANES), F32))
    if tile is not None:
        scratch.append(pltpu.VMEM((kvs[0].shape[1], tile), BF16))
    grid_spec = pltpu.PrefetchScalarGridSpec(
        num_scalar_prefetch=npre, grid=(b, nq, nk), in_specs=in_specs,
        out_specs=pl.BlockSpec((1, out_rows, out_w), lambda bi, i, j, *_: (bi, i, 0)), scratch_shapes=scratch)
    out_t = SUBLANES if stacked else q.shape[1]
    call = pl.pallas_call(functools.partial(_flash_kernel, cfg), grid_spec=grid_spec,
                          out_shape=jax.ShapeDtypeStruct((b, out_t, out_w), F32),
                          compiler_params=_cparams(3), name="attn_" + kind + ("_dec" if stacked or kv_k0 else ""))
    return call(*(([paged[0]] if paged is not None else []) + operands))


def _means(kvs, paged=None):
    if paged is not None:
        page_table, pool, row0, n_pages, per_step = paged
        b = page_table.shape[0] // n_pages
        kw, pg = pool.shape[1] // 2, pool.shape[2]
        cfg = dict(paged=True, n_kv=per_step, chunks_per_step=per_step * pg // LANES)
        in_specs = [pl.BlockSpec((1, kw, pg), lambda bi, j, pt, n=n: (row0 + pt[bi * n_pages + j * per_step + n], 0, 0))
                    for n in range(per_step)]
        operands = [page_table] + [pool] * per_step
        grid, npre = (b, n_pages // per_step), 1
    else:
        x = kvs
        b, kw = x.shape[0], x.shape[1] // 2
        cfg = dict(paged=False, n_kv=1, chunks_per_step=x.shape[2] // LANES)
        in_specs = [pl.BlockSpec((1, kw, x.shape[2]), lambda bi, j: (bi, 0, 0))]
        operands = [x]
        grid, npre = (b, 1), 0
    grid_spec = pltpu.PrefetchScalarGridSpec(
        num_scalar_prefetch=npre, grid=grid, in_specs=in_specs,
        out_specs=pl.BlockSpec((1, LANES, kw), lambda bi, j, *_: (bi, 0, 0)),
        scratch_shapes=[pltpu.VMEM((kw, LANES), F32)])
    return pl.pallas_call(functools.partial(_means_kernel, cfg), grid_spec=grid_spec,
                          out_shape=jax.ShapeDtypeStruct((b, LANES, kw), F32),
                          compiler_params=_cparams(2), name="moba_means")(*operands)


def _mix_proj_kernel(n_exp, *refs):
    it = iter(refs)
    h_ref, o1_ref, o2_ref, o3_ref, ob_ref, oc_ref, wout_ref, gffn_ref = (next(it) for _ in range(8))
    router_ref = next(it) if n_exp else None
    h2_ref, f_ref = next(it), next(it)
    w_ref = next(it) if n_exp else None
    na = H_A * HD
    oa = (o1_ref[...] + o2_ref[...] + o3_ref[...]).astype(BF16)
    h2 = (h_ref[...] + _dot(oa, wout_ref[0:na, :]) + _dot(ob_ref[...].astype(BF16), wout_ref[na:2 * na, :])
          + _dot(oc_ref[...].astype(BF16), wout_ref[2 * na:, :]))
    h2_ref[...] = h2
    f = _rms(h2, gffn_ref[...])
    f_ref[...] = f.astype(BF16)
    if n_exp:
        logits = jnp.dot(f, router_ref[...], precision=HIGHEST, preferred_element_type=F32)
        lane = lax.broadcasted_iota(I32, logits.shape, 1)
        logits = jnp.where(lane < n_exp, logits, -jnp.inf)
        v1 = jnp.max(logits, axis=-1, keepdims=True)
        i1 = jnp.min(jnp.where(logits == v1, lane, LANES), axis=-1, keepdims=True)
        rest = jnp.where(lane == i1, -jnp.inf, logits)
        v2 = jnp.max(rest, axis=-1, keepdims=True)
        i2 = jnp.min(jnp.where(rest == v2, lane, LANES), axis=-1, keepdims=True)
        e2 = jnp.exp(v2 - v1)
        w_ref[...] = jnp.where(lane == i1, 1.0 / (1.0 + e2), 0.0) + jnp.where(lane == i2, e2 / (1.0 + e2), 0.0)


def _mix_proj(h, o1, o2, o3, ob, oc, wout, gffn, router, tm):
    m, d = h.shape
    n_exp = 0 if router is None else router[1]
    row = lambda i: (i, 0)
    full = lambda i: (0, 0)
    operands = [h, o1, o2, o3, ob, oc, wout, gffn.reshape(1, d)]
    in_specs = [pl.BlockSpec((tm, d), row)] + [pl.BlockSpec((tm, x.shape[1]), row) for x in (o1, o2, o3, ob, oc)]
    in_specs += [pl.BlockSpec(wout.shape, full), pl.BlockSpec((1, d), full)]
    out_specs = [pl.BlockSpec((tm, d), row), pl.BlockSpec((tm, d), row)]
    out_shape = [jax.ShapeDtypeStruct((m, d), F32), jax.ShapeDtypeStruct((m, d), BF16)]
    if n_exp:
        operands.append(router[0])
        in_specs.append(pl.BlockSpec(router[0].shape, full))
        out_specs.append(pl.BlockSpec((tm, LANES), row))
        out_shape.append(jax.ShapeDtypeStruct((m, LANES), F32))
    return pl.pallas_call(functools.partial(_mix_proj_kernel, n_exp), grid=(m // tm,), in_specs=in_specs,
                          out_specs=out_specs, out_shape=out_shape, compiler_params=_cparams(1),
                          name="mix_proj")(*operands)


def _ffn_kernel(cfg, *refs):
    moe, final, n_exp = cfg["moe"], cfg["final"], cfg["n_exp"]
    it = iter(refs)
    h2_ref, f_ref = next(it), next(it)
    w_ref = next(it) if moe else None
    wg_ref, wu_ref, wd_ref, gple_ref, pgate_ref, pproj_ref, pemb_ref = (next(it) for _ in range(7))
    gfin_ref = next(it) if final else None
    out_ref = next(it)
    y_ref = next(it) if final else None
    acc_s = next(it)
    if moe:
        tot_s = next(it)
        e, j, nf = pl.program_id(1), pl.program_id(2), pl.num_programs(2)
    else:
        e, j, nf = 0, pl.program_id(1), pl.num_programs(1)

    if moe:
        @pl.when((j == 0) & (e == 0))
        def _zero_total():
            tot_s[...] = jnp.zeros(tot_s.shape, F32)

    @pl.when(j == 0)
    def _zero():
        acc_s[...] = jnp.zeros(acc_s.shape, F32)

    fb = f_ref[...]
    wg, wu, wd = (wg_ref[0], wu_ref[0], wd_ref[0]) if moe else (wg_ref[...], wu_ref[...], wd_ref[...])
    gg = _dot(fb, wg)
    act = (gg * _sigmoid(gg)) * _dot(fb, wu)
    acc_s[...] += _dot(act.astype(BF16), wd)

    if moe:
        @pl.when(j == nf - 1)
        def _mix():
            w = w_ref[...]
            lane = lax.broadcasted_iota(I32, w.shape, 1)
            w_e = jnp.sum(jnp.where(lane == e, w, 0.0), axis=-1, keepdims=True)
            tot_s[...] += w_e * acc_s[...]
    last = (j == nf - 1) if not moe else ((j == nf - 1) & (e == n_exp - 1))

    @pl.when(last)
    def _epilogue():
        hn = h2_ref[...] + (tot_s[...] if moe else acc_s[...])
        gate = _sigmoid(_dot(_rms(hn, gple_ref[...]).astype(BF16), pgate_ref[...]))
        h3 = hn + gate * _dot(pemb_ref[...].astype(BF16), pproj_ref[...])
        out_ref[...] = h3
        if final:
            y_ref[...] = _rms(h3, gfin_ref[...])


def _ffn(h2, f, w_route, ffn, gple, pgate, pproj, pemb, gfin, tm, tf):
    m, d = h2.shape
    moe = w_route is not None
    dff = ffn[0].shape[-1]
    final = gfin is not None
    n_exp = ffn[0].shape[0] if moe else 1
    cfg = dict(moe=moe, final=final, n_exp=n_exp)
    if moe:
        grid = (m // tm, n_exp, dff // tf)
        row = lambda i, e, j: (i, 0)
        full = lambda i, e, j: (0, 0)
        wcol = pl.BlockSpec((1, d, tf), lambda i, e, j: (e, 0, j))
        wrow = pl.BlockSpec((1, tf, d), lambda i, e, j: (e, j, 0))
    else:
        grid = (m // tm, dff // tf)
        row = lambda i, j: (i, 0)
        full = lambda i, j: (0, 0)
        wcol = pl.BlockSpec((d, tf), lambda i, j: (0, j))
        wrow = pl.BlockSpec((tf, d), lambda i, j: (j, 0))
    vec = pl.BlockSpec((1, d), full)
    operands = [h2, f]
    in_specs = [pl.BlockSpec((tm, d), row), pl.BlockSpec((tm, d), row)]
    if moe:
        operands.append(w_route)
        in_specs.append(pl.BlockSpec((tm, LANES), row))
    operands += [ffn[0], ffn[1], ffn[2], gple.reshape(1, d), pgate, pproj, pemb]
    in_specs += [wcol, wcol, wrow, vec, pl.BlockSpec(pgate.shape, full), pl.BlockSpec(pproj.shape, full),
                 pl.BlockSpec((tm, pemb.shape[1]), row)]
    out_specs = [pl.BlockSpec((tm, d), row)]
    out_shape = [jax.ShapeDtypeStruct((m, d), F32)]
    if final:
        operands.append(gfin.reshape(1, d))
        in_specs.append(vec)
        out_specs.append(pl.BlockSpec((tm, d), row))
        out_shape.append(jax.ShapeDtypeStruct((m, d), F32))
    scratch = [pltpu.VMEM((tm, d), F32)] + ([pltpu.VMEM((tm, d), F32)] if moe else [])
    res = pl.pallas_call(functools.partial(_ffn_kernel, cfg), grid=grid, in_specs=in_specs, out_specs=out_specs,
                         out_shape=out_shape, scratch_shapes=scratch, compiler_params=_cparams(len(grid)),
                         name="ffn_moe" if moe else "ffn_dense")(*operands)
    return (res[0], res[1]) if final else (res[0], None)


def _post(h, o1, o2, o3, ob, oc, wout, gffn, ffn, gple, pgate, pproj, pemb, gfin, moe, tm, tf):
    router = (ffn[3], ffn[0].shape[0]) if moe else None
    res = _mix_proj(h, o1, o2, o3, ob, oc, wout, gffn, router, tm)
    return _ffn(res[0], res[1], res[2] if moe else None, ffn, gple, pgate, pproj, pemb, gfin, tm, tf)


def _split_offsets():
    sizes = (H_A * HD, W_NSA, W_NSA, W_NSA, 3 * H_A, H_B * HD, H_B * HD, H_B * HD, H_C * DV_C, H_C * DV_C, H_C * DV_C)
    offs, s = [], 0
    for n in sizes:
        offs.append((s, s + n))
        s += n
    return offs


def _layer_weights(i, w_in, k_w1, k_w2, k_pe, v_w1, v_w2, v_pe):
    (qa, kvc, kvs, kvw, ga, qb, kb, vb, qc, kc, vc) = _split_offsets()
    w = w_in[i]
    sl = lambda r: w[:, r[0]:r[1]]
    wq = jnp.concatenate([sl(qa), sl(qb), sl(qc), jnp.pad(sl(ga), ((0, 0), (0, LANES - 3 * H_A)))], axis=1).astype(BF16)
    wkv = jnp.concatenate([sl(kvc), sl(kvs), sl(kvw), sl(kb), sl(vb), sl(kc), sl(vc)], axis=1).T.astype(BF16)
    kd = STRIDE * HD

    def cmp_w(w1, w2, pe):
        w1i = w1[i]
        w1s = jnp.concatenate([w1i[:STRIDE], w1i[STRIDE:]], axis=2)
        zero = jnp.zeros_like(w1s)
        wbd = jnp.concatenate([jnp.concatenate([w1s, zero], axis=2), jnp.concatenate([zero, w1s], axis=2)], axis=1)
        pe8 = jnp.broadcast_to(pe[i].reshape(1, 2 * kd), (SUBLANES, 2 * kd))
        return wbd.astype(BF16), w1i.reshape(2 * kd, CMP_HID).astype(BF16), pe8, w2[i].astype(BF16)

    ck, cv = cmp_w(k_w1, k_w2, k_pe), cmp_w(v_w1, v_w2, v_pe)
    cmp_weights = tuple(jnp.stack([a, b]) for a, b in zip(ck, cv))
    return wq, wkv, cmp_weights


def _from_feature_major(x_t, heads, hd):
    b, _, t = x_t.shape
    return x_t.reshape(b, 2, heads, hd, t).transpose(0, 4, 1, 2, 3)


def _pool_view(cache):
    d, n, pg, two, h, hd = cache.shape
    return cache.transpose(0, 1, 3, 4, 5, 2).reshape(d * n, two * h * hd, pg)


def _stack_rows(x, n_heads, width, col_of_head, k_width):
    b = x.shape[0]
    out = jnp.zeros((b, n_heads, SUBLANES, k_width), x.dtype)
    for h in range(n_heads):
        out = out.at[:, h, :, col_of_head(h):col_of_head(h) + width].set(x[:, :, h * width:(h + 1) * width])
    return out.reshape(b, n_heads * SUBLANES, k_width)


def kernel(x_prompt, x_sample, cache_nsa_cmp, cache_nsa_sel, cache_nsa_win, cache_moba, cache_diff, page_table, p_prompt, p_sample, g_attn, w_in, w_out, nsa_cmp_k_w1, nsa_cmp_k_w2, nsa_cmp_k_pe, nsa_cmp_v_w1, nsa_cmp_v_w2, nsa_cmp_v_pe, diff_lq1, diff_lk1, diff_lq2, diff_lk2, diff_norm_g, g_ffn, ffn_w_gate, ffn_w_up, ffn_w_down, moe_router, moe_w_gate, moe_w_up, moe_w_down, ple_gate, ple_proj, g_ple, g_final):
    depth = w_in.shape[0]
    bp, tp, d = x_prompt.shape
    bs, ts, _ = x_sample.shape
    n_pool, page = cache_nsa_cmp.shape[1], cache_nsa_cmp.shape[2]
    n_pages = page_table.shape[1]
    past_len = n_pages * page
    w_buf = cache_nsa_win.shape[2]
    per_step = 32
    assert ts <= SUBLANES and tp % 512 == 0 and n_pages % per_step == 0 and past_len % MOBA_BLOCK == 0
    pt_flat = page_table.reshape(-1).astype(I32)
    pools = {"cmp": _pool_view(cache_nsa_cmp), "sel": _pool_view(cache_nsa_sel),
             "moba": _pool_view(cache_moba), "diff": _pool_view(cache_diff)}
    win_cache_t = cache_nsa_win.transpose(0, 1, 3, 4, 5, 2).reshape(depth, bs, W_NSA, w_buf)

    tab_p = _rope_tables(jnp.arange(tp, dtype=I32))
    tab_s = _rope_tables(past_len + (jnp.arange(bs * ts, dtype=I32) % ts))

    hp = x_prompt.reshape(bp * tp, d)
    hs = x_sample.reshape(bs * ts, d)
    outs_p = {k: [] for k in ("cmp", "sel", "win", "moba", "diff")}
    outs_s = {k: [] for k in ("cmp", "sel", "win", "moba", "diff")}
    y_p = y_s = None
    n_dense = 0
    for i in range(depth):
        wq, wkv, cmpw = _layer_weights(i, w_in, nsa_cmp_k_w1, nsa_cmp_k_w2, nsa_cmp_k_pe,
                                       nsa_cmp_v_w1, nsa_cmp_v_w2, nsa_cmp_v_pe)
        lam_init = 0.8 - 0.6 * math.exp(-0.3 * i)
        dpar = jnp.zeros((SUBLANES, LANES), F32)
        for r, v in enumerate((diff_lq1, diff_lk1, diff_lq2, diff_lk2)):
            dpar = dpar.at[r, :DH_C].set(v[i].astype(F32))
        diff_par = (dpar, diff_norm_g[i].reshape(1, DV_C))
        moe = i % 2 == 1
        jf = i // 2
        w_g, w_u, w_d = ((moe_w_gate[jf], moe_w_up[jf], moe_w_down[jf]) if moe
                         else (ffn_w_gate[jf], ffn_w_up[jf], ffn_w_down[jf]))
        dff = w_g.shape[-1]
        tf = dff // 2 if (dff // 2) % LANES == 0 else dff

        ffn = (w_g.astype(BF16), w_u.astype(BF16), w_d.astype(BF16))
        if moe:
            ffn += (jnp.pad(moe_router[jf].astype(F32), ((0, 0), (0, LANES - moe_router.shape[2]))),)
        wout = w_out[i].astype(BF16)
        pgate, pproj = ple_gate[i].astype(BF16), ple_proj[i].astype(BF16)
        gfin = g_final if i == depth - 1 else None

        zq, cmp_t, sel_t, win_t, moba_t, diff_t = _proj_in(hp, g_attn[i], wq, wkv, tab_p, bp, tp, 512)
        for k, v in zip(("cmp", "sel", "win", "moba", "diff"), (cmp_t, sel_t, win_t, moba_t, diff_t)):
            outs_p[k].append(v)
        zq3 = zq.reshape(bp, tp, NQ)
        n_cmp = (tp - L_CMP) // STRIDE + 1
        kvcmp = _compress_tail(_half_proj(cmp_t, cmpw[0]), *cmpw[1:])
        tq = 512
        o_cmp, bm = _cmp_attn(zq3, kvcmp, n_cmp, tp // L_SEL, 0, tq)
        ta = 512
        qa_blk, qb_blk, qc_blk = (ta, H_A * HD, 0), (ta, H_B * HD, 1), (ta, H_C * DV_C, QC0 // (H_C * DV_C))
        o_sel = _flash("sel", zq3, [sel_t], t0=0, out_w=H_A * HD, stacked=False, q_block=qa_blk, gates=zq3, bm=bm, tile=512)
        o_win = _flash("win", zq3, [win_t], t0=0, out_w=H_A * HD, stacked=False, q_block=qa_blk, gates=zq3, tile=512)
        means = _means(moba_t)
        o_b = _flash("moba", zq3, [moba_t], t0=0, out_w=H_B * HD, stacked=False, q_block=qb_blk, means=means,
                     tile=512, n_blk=tp // MOBA_BLOCK)
        o_c = _flash("diff", zq3, [diff_t], t0=0, out_w=H_C * DV_C, stacked=False, q_block=qc_blk, diff_par=diff_par,
                     tile=512, lam_init=lam_init)
        flat = lambda x: x.reshape(bp * tp, x.shape[-1])
        hp, y = _post(hp, flat(o_cmp), flat(o_sel), flat(o_win), flat(o_b), flat(o_c), wout, g_ffn[i], ffn,
                      g_ple[i], pgate, pproj, p_prompt[i].reshape(bp * tp, -1), gfin, moe, 512, tf)
        y_p = y if y is not None else y_p

        ms = bs * ts
        zq, cmp_n, sel_n, win_n, moba_n, diff_n = _proj_in(hs, g_attn[i], wq, wkv, tab_s, 1, ms, ms)
        news = dict(cmp=cmp_n, sel=sel_n, win=win_n, moba=moba_n, diff=diff_n)
        for k, v in news.items():
            outs_s[k].append(v)

        def tail_of(x):
            w = x.shape[1]
            return jnp.pad(x.reshape(w, bs, ts).transpose(1, 0, 2), ((0, 0), (0, 0), (0, LANES - ts)))

        zq8 = jnp.pad(zq.reshape(bs, ts, NQ), ((0, 0), (0, SUBLANES - ts), (0, 0)))
        l_tot = past_len + ts
        n_cmp = (l_tot - L_CMP) // STRIDE + 1
        assert (n_cmp + 1) * STRIDE == past_len
        half = _half_proj(None, cmpw[0], paged=(pt_flat, pools["cmp"], i * n_pool, n_pages, per_step))
        kvcmp = _compress_tail(half, *cmpw[1:])
        n_sel = -(-l_tot // L_SEL)
        o_cmp, bm = _cmp_attn(zq8, kvcmp, n_cmp, n_sel, past_len, SUBLANES)
        q_sel = _stack_rows(zq8[:, :, QA0:QA0 + H_A * HD], H_A, HD, lambda h: (h // HG_A) * HD, G_A * HD)
        bm_rows = jnp.repeat(bm, HG_A, axis=1).reshape(bs, H_A * SUBLANES, bm.shape[3])
        blk_step = per_step * page // L_SEL
        n_steps = n_pages // per_step
        assert n_sel == n_steps * blk_step + 1 and blk_step <= LANES
        bm_steps = bm_rows[:, :, :n_steps * blk_step].reshape(bs, H_A * SUBLANES, n_steps, blk_step).transpose(0, 2, 1, 3)
        bm_st = jnp.concatenate([jnp.pad(bm_steps, ((0, 0), (0, 0), (0, 0), (0, LANES - blk_step))),
                                 jnp.pad(bm_rows[:, None, :, n_steps * blk_step:n_sel],
                                         ((0, 0), (0, 0), (0, 0), (0, LANES - 1)))], axis=1)
        ga8 = zq8
        o_sel = _flash("sel", q_sel, None, t0=past_len, out_w=H_A * HD, stacked=True,
                       q_block=(H_A * SUBLANES, G_A * HD, 0), gates=ga8, bm=bm_st,
                       paged=(pt_flat, pools["sel"], i * n_pool, n_pages, per_step), tail=tail_of(sel_n))
        qa8_blk = (SUBLANES, H_A * HD, 0)
        o_win = _flash("win", zq8, [win_cache_t[i], tail_of(win_n)], t0=past_len, out_w=H_A * HD, stacked=False,
                       q_block=qa8_blk, gates=ga8, kv_k0=[past_len - w_buf, past_len])
        means = _means(None, paged=(pt_flat, pools["moba"], i * n_pool, n_pages, per_step))
        q_moba = _stack_rows(zq8[:, :, QB0:QB0 + H_B * HD], H_B, HD, lambda h: h * HD, H_B * HD)
        o_b = _flash("moba", q_moba, None, t0=past_len, out_w=H_B * HD, stacked=True,
                     q_block=(H_B * SUBLANES, H_B * HD, 0), means=means,
                     paged=(pt_flat, pools["moba"], i * n_pool, n_pages, per_step), tail=tail_of(moba_n),
                     n_blk=-(-l_tot // MOBA_BLOCK))
        q_diff = _stack_rows(zq8[:, :, QC0:QC0 + H_C * DV_C], 2 * H_C, DH_C, lambda u: u * DH_C, H_C * DV_C)
        o_c = _flash("diff", q_diff, None, t0=past_len, out_w=H_C * DV_C, stacked=True,
                     q_block=(2 * H_C * SUBLANES, H_C * DV_C, 0), diff_par=diff_par,
                     paged=(pt_flat, pools["diff"], i * n_pool, n_pages, per_step), tail=tail_of(diff_n),
                     lam_init=lam_init)
        flat = lambda x: x[:, :ts].reshape(ms, x.shape[-1])
        hs, y = _post(hs, flat(o_cmp), flat(o_sel), flat(o_win), flat(o_b), flat(o_c), wout, g_ffn[i], ffn,
                      g_ple[i], pgate, pproj, p_sample[i].reshape(ms, -1), gfin, moe, ms, tf)
        y_s = y if y is not None else y_s

    heads = dict(cmp=(G_A, HD), sel=(G_A, HD), win=(G_A, HD), moba=(H_B, HD), diff=(H_C, DV_C))
    res_p, res_s = {}, {}
    for k, (nh, hd) in heads.items():
        res_p[k] = jnp.stack([_from_feature_major(x, nh, hd) for x in outs_p[k]], axis=0)
        res_s[k] = jnp.stack([x[0].T.reshape(bs, ts, 2, nh, hd) for x in outs_s[k]], axis=0)
    keep = min(WINDOW, tp)
    win_p = res_p["win"][:, :, tp - keep:]
    win_all = jnp.concatenate([cache_nsa_win, res_s["win"]], axis=2)
    win_s = win_all[:, :, win_all.shape[2] - min(WINDOW, win_all.shape[2]):]
    return (y_p.reshape(bp, tp, d), y_s.reshape(bs, ts, d), res_p["cmp"], res_p["sel"], win_p, res_p["moba"],
            res_p["diff"], res_s["cmp"], res_s["sel"], win_s, res_s["moba"], res_s["diff"])
```
